```python
import math
import jax, jax.numpy as jnp
from jax import lax
import numpy as np

D_MODEL = 1024
BATCH = 8
SEQ = 4096
DEPTH = 1
DEC_BATCH = 4
DEC_SEQ = 8192
PAST_LEN = 128

HEAD_DIM = 64
N_HEADS_DIL = 8
DIL_WIDTH = N_HEADS_DIL * HEAD_DIM
N_HEADS_GLA = 4
GLA_DK = 64
GLA_DV = 128
GLA_KEY_WIDTH = N_HEADS_GLA * GLA_DK
GLA_WIDTH = N_HEADS_GLA * GLA_DV
GLA_RANK = 16
GLA_TAU = 16.0
GLA_CHUNK = 64
DILATED_PATTERNS = ((128, 1), (512, 4), (2048, 16))
ROT_DIM = HEAD_DIM // 4
ROPE_THETA = 500000.0
D_FF = ((8 * D_MODEL // 3 + 255) // 256) * 256
EPS = 1e-6
NEG_INF = -1e30
PROJ_SPLITS = (DIL_WIDTH, DIL_WIDTH, DIL_WIDTH, GLA_KEY_WIDTH, GLA_KEY_WIDTH, GLA_WIDTH, GLA_WIDTH, 2 * GLA_RANK)
PROJ_WIDTH = sum(PROJ_SPLITS)
PROJ_OFFSETS = tuple(int(o) for o in np.cumsum(PROJ_SPLITS)[:-1])

kernel_name = "hybrid_gla_dilated_encoder"


def _rmsnorm(x, g):
    xf = x.astype(jnp.float32)
    y = xf * lax.rsqrt(jnp.mean(xf * xf, axis=-1, keepdims=True) + EPS)
    return (y * g.astype(jnp.float32)).astype(x.dtype)


def _partial_rotary(x):
    S = x.shape[1]
    half = ROT_DIM // 2
    inv_freq = jnp.power(jnp.float32(ROPE_THETA), -jnp.arange(0, ROT_DIM, 2, dtype=jnp.float32) / ROT_DIM)
    ang = jnp.arange(S, dtype=jnp.float32)[:, None] * inv_freq[None, :]
    cos = jnp.cos(ang)[None, :, None, :]
    sin = jnp.sin(ang)[None, :, None, :]
    rot = x[..., :ROT_DIM].astype(jnp.float32)
    x1, x2 = rot[..., :half], rot[..., half:]
    rot = jnp.concatenate([x1 * cos - x2 * sin, x1 * sin + x2 * cos], axis=-1)
    return jnp.concatenate([rot.astype(x.dtype), x[..., ROT_DIM:]], axis=-1)


def _to_strided(x, dil):
    B, S = x.shape[0], x.shape[1]
    rest = x.shape[2:]
    L = S // dil
    return jnp.swapaxes(x.reshape((B, L, dil) + rest), 1, 2).reshape((B * dil, L) + rest)


def _from_strided(x, B, dil):
    L = x.shape[1]
    rest = x.shape[2:]
    return jnp.swapaxes(x.reshape((B, dil, L) + rest), 1, 2).reshape((B, L * dil) + rest)


def _banded_attention(q, k, v, half):
    N, L, H, D = q.shape
    blk = half
    nb = -(-L // blk)
    Lp = nb * blk
    qb = jnp.pad(q, ((0, 0), (0, Lp - L), (0, 0), (0, 0))).reshape(N, nb, blk, H, D)

    def windows(t):
        tp = jnp.pad(t, ((0, 0), (blk, blk + Lp - L), (0, 0), (0, 0))).reshape(N, nb + 2, blk, H, D)
        return jnp.concatenate([tp[:, :-2], tp[:, 1:-1], tp[:, 2:]], axis=2)

    kw, vw = windows(k), windows(v)
    s = jnp.einsum('nbqhd,nbkhd->nbhqk', qb, kw, preferred_element_type=jnp.float32)
    blocks = jnp.arange(nb)[:, None] * blk
    qpos = blocks + jnp.arange(blk)[None, :]
    kpos = blocks - blk + jnp.arange(3 * blk)[None, :]
    rel = kpos[:, None, :] - qpos[:, :, None]
    valid = (jnp.abs(rel) <= half) & (kpos[:, None, :] >= 0) & (kpos[:, None, :] < L)
    s = jnp.where(valid[None, :, None], s, NEG_INF)
    lse = jax.nn.logsumexp(s, axis=-1)
    p = jnp.exp(s - lse[..., None])
    o = jnp.einsum('nbhqk,nbkhd->nbqhd', p.astype(v.dtype), vw).reshape(N, Lp, H, D)[:, :L]
    lse = jnp.transpose(lse, (0, 1, 3, 2)).reshape(N, Lp, H)[:, :L]
    return o, lse


def _dilated_mixture(q, k, v):
    B = q.shape[0]
    outs, lses = [], []
    for window, dil in DILATED_PATTERNS:
        half = window // (2 * dil)
        o, l = _banded_attention(_to_strided(q, dil), _to_strided(k, dil), _to_strided(v, dil), half)
        outs.append(_from_strided(o, B, dil))
        lses.append(_from_strided(l, B, dil))
    w = jax.nn.softmax(jnp.stack(lses), axis=0)
    o = jnp.einsum('gbsh,gbshd->bshd', w, jnp.stack(outs).astype(jnp.float32))
    return o.astype(q.dtype)


def _gla_direction(q, k, v, log_a, inclusive):
    B, S, H, dk = q.shape
    dv = v.shape[-1]
    C = GLA_CHUNK
    nc = S // C
    f32 = jnp.float32
    q = q.astype(f32).reshape(B, nc, C, H, dk)
    k = k.astype(f32).reshape(B, nc, C, H, dk)
    v = v.astype(f32).reshape(B, nc, C, H, dv)
    b = jnp.cumsum(log_a.astype(f32).reshape(B, nc, C, H, dk), axis=2)
    q_in = q * jnp.exp(b)
    k_in = k * jnp.exp(-b)
    mask = jnp.tril(jnp.ones((C, C), dtype=bool), k=0 if inclusive else -1)
    att = jnp.where(mask, jnp.einsum('bnthk,bnshk->bnhts', q_in, k_in), 0.0)
    o = jnp.einsum('bnhts,bnshv->bnthv', att, v)
    b_last = b[:, :, -1:]
    kv = jnp.einsum('bnshk,bnshv->nbhkv', k * jnp.exp(b_last - b), v)
    decay = jnp.transpose(jnp.exp(b_last[:, :, 0]), (1, 0, 2, 3))

    def step(state, inp):
        d, u = inp
        return d[..., None] * state + u, state

    _, states = lax.scan(step, jnp.zeros((B, H, dk, dv), f32), (decay, kv))
    o = o + jnp.einsum('bnthk,nbhkv->bnthv', q_in, states)
    return o.reshape(B, S, H, dv)


def _gla_mixer(q, k, v, r, g_lr, w_gate_f, b_gate_f, w_gate_b, b_gate_b, g_gla):
    B, S = q.shape[0], q.shape[1]
    q = q.reshape(B, S, N_HEADS_GLA, GLA_DK) * (GLA_DK ** -0.5)
    k = k.reshape(B, S, N_HEADS_GLA, GLA_DK)
    v = v.reshape(B, S, N_HEADS_GLA, GLA_DV)
    gf, gb = g_lr[..., :GLA_RANK], g_lr[..., GLA_RANK:]
    log_af = (jax.nn.log_sigmoid((gf @ w_gate_f + b_gate_f).astype(jnp.float32)) / GLA_TAU).reshape(B, S, N_HEADS_GLA, GLA_DK)
    log_ab = (jax.nn.log_sigmoid((gb @ w_gate_b + b_gate_b).astype(jnp.float32)) / GLA_TAU).reshape(B, S, N_HEADS_GLA, GLA_DK)
    o_f = _gla_direction(q, k, v, log_af, True)
    flip = lambda t: jnp.flip(t, axis=1)
    o_b = flip(_gla_direction(flip(q), flip(k), flip(v), flip(log_ab), False))
    o = _rmsnorm((o_f + o_b).astype(r.dtype), g_gla)
    return o.reshape(B, S, GLA_WIDTH) * jax.nn.silu(r)


def _encoder_layer(x, g_mix, w_in, w_gate_f, b_gate_f, w_gate_b, b_gate_b, g_gla, w_out,
                   g_ffn, w_ffn_gate, w_ffn_up, w_ffn_down):
    B, S, _ = x.shape
    h = _rmsnorm(x, g_mix)
    proj = h @ w_in
    qd, kd, vd, qa, ka, va, ra, g_lr = jnp.split(proj, PROJ_OFFSETS, axis=-1)
    hs = (B, S, N_HEADS_DIL, HEAD_DIM)
    qd = _partial_rotary(qd.reshape(hs)) * (HEAD_DIM ** -0.5)
    kd = _partial_rotary(kd.reshape(hs))
    vd = vd.reshape(hs)
    o_dil = _dilated_mixture(qd, kd, vd).reshape(B, S, DIL_WIDTH)
    o_gla = _gla_mixer(qa, ka, va, ra, g_lr, w_gate_f, b_gate_f, w_gate_b, b_gate_b, g_gla)
    x = x + jnp.concatenate([o_dil, o_gla], axis=-1) @ w_out
    h = _rmsnorm(x, g_ffn)
    x = x + (jax.nn.silu(h @ w_ffn_gate) * (h @ w_ffn_up)) @ w_ffn_down
    return x


def _trunk(x, g_mix, w_in, w_gate_f, b_gate_f, w_gate_b, b_gate_b, g_gla, w_out,
           g_ffn, w_ffn_gate, w_ffn_up, w_ffn_down, g_final):
    for l in range(DEPTH):
        x = _encoder_layer(x, g_mix[l], w_in[l], w_gate_f[l], b_gate_f[l], w_gate_b[l], b_gate_b[l],
                           g_gla[l], w_out[l], g_ffn[l], w_ffn_gate[l], w_ffn_up[l], w_ffn_down[l])
    return _rmsnorm(x, g_final)


def setup_inputs(seed: int = 0) -> dict:
    key = jax.random.key(seed)
    ks = jax.random.split(key, 16)
    f32 = jnp.float32
    nrm = lambda k, shape, scale: jax.random.normal(k, shape, f32) * scale
    gain = lambda k, n: jnp.ones((DEPTH, n), f32) + nrm(k, (DEPTH, n), 0.02)
    return {
        "x_prompt": nrm(ks[0], (BATCH, SEQ, D_MODEL), 1.0),
        "x_sample": nrm(ks[1], (DEC_BATCH, DEC_SEQ, D_MODEL), 1.0),
        "g_mix": gain(ks[2], D_MODEL),
        "w_in": nrm(ks[3], (DEPTH, D_MODEL, PROJ_WIDTH), D_MODEL ** -0.5),
        "w_gate_f": nrm(ks[4], (DEPTH, GLA_RANK, GLA_KEY_WIDTH), GLA_RANK ** -0.5),
        "b_gate_f": nrm(ks[5], (DEPTH, GLA_KEY_WIDTH), 0.1),
        "w_gate_b": nrm(ks[6], (DEPTH, GLA_RANK, GLA_KEY_WIDTH), GLA_RANK ** -0.5),
        "b_gate_b": nrm(ks[7], (DEPTH, GLA_KEY_WIDTH), 0.1),
        "g_gla": gain(ks[8], GLA_DV),
        "w_out": nrm(ks[9], (DEPTH, D_MODEL, D_MODEL), D_MODEL ** -0.5),
        "g_ffn": gain(ks[10], D_MODEL),
        "w_ffn_gate": nrm(ks[11], (DEPTH, D_MODEL, D_FF), D_MODEL ** -0.5),
        "w_ffn_up": nrm(ks[12], (DEPTH, D_MODEL, D_FF), D_MODEL ** -0.5),
        "w_ffn_down": nrm(ks[13], (DEPTH, D_FF, D_MODEL), D_FF ** -0.5),
        "g_final": jnp.ones((D_MODEL,), f32) + nrm(ks[14], (D_MODEL,), 0.02),
    }


def reference(x_prompt, x_sample, g_mix, w_in, w_gate_f, b_gate_f, w_gate_b, b_gate_b, g_gla, w_out,
              g_ffn, w_ffn_gate, w_ffn_up, w_ffn_down, g_final):
    y_prompt = _trunk(x_prompt, g_mix, w_in, w_gate_f, b_gate_f, w_gate_b, b_gate_b, g_gla, w_out,
                      g_ffn, w_ffn_gate, w_ffn_up, w_ffn_down, g_final)
    y_sample = _trunk(x_sample, g_mix, w_in, w_gate_f, b_gate_f, w_gate_b, b_gate_b, g_gla, w_out,
                      g_ffn, w_ffn_gate, w_ffn_up, w_ffn_down, g_final)
    return (y_prompt, y_sample)
```

```python
import functools

import jax
import jax.numpy as jnp
from jax import lax
from jax.experimental import pallas as pl
from jax.experimental.pallas import tpu as pltpu

D_MODEL = 1024
HEAD_DIM = 64
N_HEADS_DIL = 8
DIL_WIDTH = N_HEADS_DIL * HEAD_DIM
N_HEADS_GLA = 4
GLA_DK = 64
GLA_DV = 128
GLA_KEY_WIDTH = N_HEADS_GLA * GLA_DK
GLA_WIDTH = N_HEADS_GLA * GLA_DV
GLA_RANK = 16
GLA_TAU = 16.0
GLA_CHUNK = 64
DILATED_PATTERNS = ((128, 1), (512, 4), (2048, 16))
ROT_DIM = HEAD_DIM // 4
ROPE_THETA = 500000.0
D_FF = 2816
EPS = 1e-6
NEG_INF = -1e30
PROJ_WIDTH = 3 * DIL_WIDTH + 2 * GLA_KEY_WIDTH + 2 * GLA_WIDTH + 2 * GLA_RANK
QKV_WIDTH = 3 * DIL_WIDTH

LANES = 128
BAND_HALF = 64
Q_TILE = 128
K_WINDOW = Q_TILE + 2 * BAND_HALF
VMEM_LIMIT = 56 * 1024 * 1024

F32 = jnp.float32
BF16 = jnp.bfloat16


def _rms(x, g):
    return x * lax.rsqrt(jnp.mean(x * x, axis=-1, keepdims=True) + EPS) * g


def _inproj_kernel(x_ref, g_ref, w_ref, wg_ref, bg_ref, cos_ref, sa_ref, sb_ref,
                   qkv_ref, gq_ref, gk_ref, gv_ref, gr_ref, la_ref):
    h = _rms(x_ref[0], g_ref[...]).astype(BF16)

    def proj(lo, hi):
        return jnp.dot(h, w_ref[:, lo:hi], preferred_element_type=F32)

    cos_t, sin_a, sin_b = cos_ref[...], sa_ref[...], sb_ref[...]

    def rotary_store(t, col0, scale):
        for c in range(DIL_WIDTH // LANES):
            xb = t[:, c * LANES:(c + 1) * LANES]
            yb = (xb * cos_t + pltpu.roll(xb, LANES - ROT_DIM // 2, 1) * sin_a
                  + pltpu.roll(xb, ROT_DIM // 2, 1) * sin_b)
            if scale is not None:
                yb = yb * scale
            qkv_ref[0, :, col0 + c * LANES:col0 + (c + 1) * LANES] = yb.astype(BF16)

    o = 0
    rotary_store(proj(o, o + DIL_WIDTH), 0, HEAD_DIM ** -0.5)
    o += DIL_WIDTH
    rotary_store(proj(o, o + DIL_WIDTH), DIL_WIDTH, None)
    o += DIL_WIDTH
    qkv_ref[0, :, 2 * DIL_WIDTH:] = proj(o, o + DIL_WIDTH).astype(BF16)
    o += DIL_WIDTH
    gq_ref[0] = proj(o, o + GLA_KEY_WIDTH) * (GLA_DK ** -0.5)
    o += GLA_KEY_WIDTH
    gk_ref[0] = proj(o, o + GLA_KEY_WIDTH)
    o += GLA_KEY_WIDTH
    gv_ref[0] = proj(o, o + GLA_WIDTH).astype(BF16)
    o += GLA_WIDTH
    gr_ref[0] = proj(o, o + GLA_WIDTH)
    o += GLA_WIDTH
    g_lr = proj(o, o + 2 * GLA_RANK).astype(BF16)
    pre = jnp.dot(g_lr, wg_ref[...], preferred_element_type=F32) + bg_ref[...]
    la_ref[0] = (jnp.minimum(pre, 0.0) - jnp.log1p(jnp.exp(-jnp.abs(pre)))) * (1.0 / GLA_TAU)


def _rotary_tables(S):
    half = ROT_DIM // 2
    inv_freq = jnp.power(jnp.float32(ROPE_THETA), -jnp.arange(0, ROT_DIM, 2, dtype=F32) / ROT_DIM)
    ang = jnp.arange(S, dtype=F32)[:, None] * inv_freq[None, :]
    cos, sin = jnp.cos(ang), jnp.sin(ang)
    ones = jnp.ones((S, HEAD_DIM - ROT_DIM), F32)
    zeros = jnp.zeros((S, HEAD_DIM - ROT_DIM), F32)
    zh = jnp.zeros((S, half), F32)
    cos_t = jnp.concatenate([cos, cos, ones], axis=1)
    sin_a = jnp.concatenate([-sin, zh, zeros], axis=1)
    sin_b = jnp.concatenate([zh, sin, zeros], axis=1)
    rep = LANES // HEAD_DIM
    return tuple(jnp.tile(t, (1, rep)) for t in (cos_t, sin_a, sin_b))


def _inproj(x, g_mix, w_in, w_gate, b_gate, tm):
    B, S, D = x.shape
    cos_t, sin_a, sin_b = _rotary_tables(S)
    tok = lambda w: pl.BlockSpec((1, tm, w), lambda b, i: (b, i, 0))
    full = lambda a: pl.BlockSpec(a.shape, lambda b, i: (0,) * a.ndim)
    tab = pl.BlockSpec((tm, LANES), lambda b, i: (i, 0))
    out_shape = (
        jax.ShapeDtypeStruct((B, S, QKV_WIDTH), BF16),
        jax.ShapeDtypeStruct((B, S, GLA_KEY_WIDTH), F32),
        jax.ShapeDtypeStruct((B, S, GLA_KEY_WIDTH), F32),
        jax.ShapeDtypeStruct((B, S, GLA_WIDTH), BF16),
        jax.ShapeDtypeStruct((B, S, GLA_WIDTH), F32),
        jax.ShapeDtypeStruct((B, S, 2 * GLA_KEY_WIDTH), F32),
    )
    return pl.pallas_call(
        _inproj_kernel,
        grid=(B, S // tm),
        in_specs=[tok(D), full(g_mix), full(w_in), full(w_gate), full(b_gate), tab, tab, tab],
        out_specs=tuple(tok(s.shape[-1]) for s in out_shape),
        out_shape=out_shape,
        compiler_params=pltpu.CompilerParams(
            dimension_semantics=("parallel", "parallel"), vmem_limit_bytes=VMEM_LIMIT),
        name="inproj",
    )(x, g_mix, w_in, w_gate, b_gate, cos_t, sin_a, sin_b)


def _dil_kernel(*refs, L, TL, has_prev, final):
    q_ref, kl_ref, km_ref, kr_ref, vl_ref, vm_ref, vr_ref = refs[:7]
    refs = refs[7:]
    if has_prev:
        op_ref, lp_ref = refs[:2]
        refs = refs[2:]
    if final:
        o_ref, kwin, vwin = refs
    else:
        o_ref, l_ref, kwin, vwin = refs
    i = pl.program_id(2)

    kwin[0:BAND_HALF] = kl_ref[0]
    kwin[BAND_HALF:BAND_HALF + TL] = km_ref[0]
    kwin[BAND_HALF + TL:] = kr_ref[0]
    vwin[0:BAND_HALF] = vl_ref[0]
    vwin[BAND_HALF:BAND_HALF + TL] = vm_ref[0]
    vwin[BAND_HALF + TL:] = vr_ref[0]

    lane = lax.broadcasted_iota(jnp.int32, (1, LANES), 1)
    head0 = lane < HEAD_DIM
    m0 = head0.astype(BF16)
    m1 = (1.0 - head0.astype(F32)).astype(BF16)
    row = lax.broadcasted_iota(jnp.int32, (Q_TILE, K_WINDOW), 0)
    col = lax.broadcasted_iota(jnp.int32, (Q_TILE, K_WINDOW), 1)
    band = (col >= row) & (col <= row + 2 * BAND_HALF)

    for j in range(TL // Q_TILE):
        r0 = j * Q_TILE
        kpos = i * TL + (r0 - BAND_HALF) + col
        valid = band & (kpos >= 0) & (kpos < L)
        bias = jnp.where(valid, 0.0, NEG_INF)
        bias2 = jnp.concatenate([bias, bias], axis=0)
        for hp in range(DIL_WIDTH // LANES):
            cs = slice(hp * LANES, (hp + 1) * LANES)
            q = q_ref[0, r0:r0 + Q_TILE, cs]
            k = kwin[r0:r0 + K_WINDOW, cs]
            v = vwin[r0:r0 + K_WINDOW, cs]
            qq = jnp.concatenate([q * m0, q * m1], axis=0)
            s = lax.dot_general(qq, k, (((1,), (1,)), ((), ())),
                                preferred_element_type=F32) + bias2
            m = jnp.max(s, axis=1, keepdims=True)
            p = jnp.exp(s - m)
            l = jnp.sum(p, axis=1, keepdims=True)
            pv = jnp.dot(p.astype(BF16), v, preferred_element_type=F32)
            acc = jnp.where(head0, pv[:Q_TILE], pv[Q_TILE:])
            mm = jnp.where(head0, m[:Q_TILE], m[Q_TILE:])
            ll = jnp.where(head0, l[:Q_TILE], l[Q_TILE:])
            if has_prev:
                op = op_ref[0, r0:r0 + Q_TILE, cs]
                lp = lp_ref[0, r0:r0 + Q_TILE, cs]
                mn = jnp.maximum(lp, mm)
                w1 = jnp.exp(lp - mn)
                w2 = jnp.exp(mm - mn)
                den = w1 + ll * w2
                o = (op * w1 + acc * w2) / den
                lse = mn + jnp.log(den)
            else:
                o = acc / ll
                lse = mm + jnp.log(ll)
            o_ref[0, r0:r0 + Q_TILE, cs] = o.astype(o_ref.dtype)
            if not final:
                l_ref[0, r0:r0 + Q_TILE, cs] = lse


def _dilated_pass(qkv, prev, dil, final):
    B, S, _ = qkv.shape
    L = S // dil
    TL = min(512, L)
    nb = TL // BAND_HALF
    qkv_v = qkv.reshape(B, L, dil * QKV_WIDTH)
    W = DIL_WIDTH
    main = lambda t: pl.BlockSpec((1, TL, W), lambda b, r, i: (b, i, 3 * r + t))
    left = lambda t: pl.BlockSpec(
        (1, BAND_HALF, W), lambda b, r, i: (b, jnp.maximum(i * nb - 1, 0), 3 * r + t))
    right = lambda t: pl.BlockSpec(
        (1, BAND_HALF, W), lambda b, r, i: (b, jnp.minimum((i + 1) * nb, L // BAND_HALF - 1), 3 * r + t))
    state = pl.BlockSpec((1, TL, W), lambda b, r, i: (b, i, r))
    in_specs = [main(0), left(1), main(1), right(1), left(2), main(2), right(2)]
    args = [qkv_v] * 7
    if prev is not None:
        in_specs += [state, state]
        args += [prev[0].reshape(B, L, dil * W), prev[1].reshape(B, L, dil * W)]
    if final:
        out_shape = jax.ShapeDtypeStruct((B, L, dil * W), BF16)
        out_specs = state
    else:
        out_shape = (jax.ShapeDtypeStruct((B, L, dil * W), F32),) * 2
        out_specs = (state, state)
    out = pl.pallas_call(
        functools.partial(_dil_kernel, L=L, TL=TL, has_prev=prev is not None, final=final),
        grid=(B, dil, L // TL),
        in_specs=in_specs,
        out_specs=out_specs,
        out_shape=out_shape,
        scratch_shapes=[pltpu.VMEM((TL + 2 * BAND_HALF, W), BF16)] * 2,
        compiler_params=pltpu.CompilerParams(
            dimension_semantics=("parallel", "parallel", "parallel"), vmem_limit_bytes=VMEM_LIMIT),
        name=f"dilated_d{dil}",
    )(*args)
    if final:
        return out.reshape(B, S, W)
    return tuple(o.reshape(B, S, W) for o in out)


def _dilated_mixture(qkv):
    state = None
    for n, (window, dil) in enumerate(DILATED_PATTERNS):
        assert window // (2 * dil) == BAND_HALF
        state = _dilated_pass(qkv, state, dil, final=n == len(DILATED_PATTERNS) - 1)
    return state


def _gla_kernel(*refs, TS, forward):
    if forward:
        q_ref, k_ref, la_ref, v_ref, ob_ref, r_ref, g_ref, o_ref, state_ref = refs
    else:
        q_ref, k_ref, la_ref, v_ref, o_ref, state_ref = refs
    C = GLA_CHUNK
    nch = TS // C

    @pl.when(pl.program_id(2) == 0)
    def _():
        state_ref[...] = jnp.zeros_like(state_ref)

    lane = lax.broadcasted_iota(jnp.int32, (1, LANES), 1)
    m0 = (lane < GLA_DK).astype(F32)
    m1 = 1.0 - m0
    tr = lax.broadcasted_iota(jnp.int32, (C, C), 0)
    tc = lax.broadcasted_iota(jnp.int32, (C, C), 1)
    if forward:
        cum = tc <= tr
        keep = tc <= tr
    else:
        cum = tc >= tr
        keep = tc > tr
    cum_m = cum.astype(BF16)
    keep2 = jnp.concatenate([keep, keep], axis=0)

    state_t = state_ref[...]
    order = range(nch) if forward else range(nch - 1, -1, -1)
    for c in order:
        rows = slice(c * C, (c + 1) * C)
        la = la_ref[0, rows, :]
        h1 = la.astype(BF16)
        r1 = la - h1.astype(F32)
        h2 = r1.astype(BF16)
        h3 = (r1 - h2.astype(F32)).astype(BF16)
        bb = jnp.dot(cum_m, jnp.concatenate([h1, h2, h3], axis=1), preferred_element_type=F32)
        b = bb[:, :LANES] + bb[:, LANES:2 * LANES] + bb[:, 2 * LANES:]
        total = b[C - 1:C, :] if forward else b[0:1, :]
        q = q_ref[0, rows, :]
        k = k_ref[0, rows, :]
        v = v_ref[0, rows, :]
        q_in = q * jnp.exp(b)
        k_in = (k * jnp.exp(-b)).astype(BF16)
        k_dec = k * jnp.exp(total - b)
        qs = jnp.concatenate([q_in * m0, q_in * m1], axis=0).astype(BF16)
        att = lax.dot_general(qs, k_in, (((1,), (1,)), ((), ())), preferred_element_type=F32)
        att = jnp.where(keep2, att, 0.0).astype(BF16)
        intra = jnp.dot(att, v, preferred_element_type=F32)
        inter = lax.dot_general(qs, state_t.astype(BF16), (((1,), (1,)), ((), ())),
                                preferred_element_type=F32)
        o = jnp.concatenate([intra[:C, :GLA_DV] + inter[:C], intra[C:, GLA_DV:] + inter[C:]], axis=1)
        ks = jnp.concatenate([k_dec * m0, k_dec * m1], axis=0).astype(BF16)
        vs = jnp.concatenate([v[:, :GLA_DV], v[:, GLA_DV:]], axis=0)
        kv_t = lax.dot_general(vs, ks, (((0,), (0,)), ((), ())), preferred_element_type=F32)
        state_t = state_t * jnp.exp(total) + kv_t
        if forward:
            tot = o + ob_ref[0, rows, :]
            r = r_ref[0, rows, :]
            gate = r / (1.0 + jnp.exp(-r))
            for hh in range(2):
                hs = slice(hh * GLA_DV, (hh + 1) * GLA_DV)
                o_ref[0, rows, hs] = (_rms(tot[:, hs], g_ref[...]) * gate[:, hs]).astype(o_ref.dtype)
        else:
            o_ref[0, rows, :] = o
    state_ref[...] = state_t


def _gla_pass(gq, gk, la, gv, ts, forward, ob=None, gr=None, g_gla=None):
    B, S, _ = gq.shape
    nt = S // ts
    tile = (lambda i: i) if forward else (lambda i: nt - 1 - i)
    la_off = 0 if forward else GLA_KEY_WIDTH // LANES
    key = pl.BlockSpec((1, ts, LANES), lambda b, h, i: (b, tile(i), h))
    gate = pl.BlockSpec((1, ts, LANES), lambda b, h, i: (b, tile(i), la_off + h))
    val = pl.BlockSpec((1, ts, 2 * GLA_DV), lambda b, h, i: (b, tile(i), h))
    in_specs = [key, key, gate, val]
    args = [gq, gk, la, gv]
    if forward:
        in_specs += [val, val, pl.BlockSpec(g_gla.shape, lambda b, h, i: (0, 0))]
        args += [ob, gr, g_gla]
    return pl.pallas_call(
        functools.partial(_gla_kernel, TS=ts, forward=forward),
        grid=(B, N_HEADS_GLA // 2, nt),
        in_specs=in_specs,
        out_specs=val,
        out_shape=jax.ShapeDtypeStruct((B, S, GLA_WIDTH), BF16 if forward else F32),
        scratch_shapes=[pltpu.VMEM((GLA_DV, 2 * GLA_DK), F32)],
        compiler_params=pltpu.CompilerParams(
            dimension_semantics=("parallel", "parallel", "arbitrary"), vmem_limit_bytes=VMEM_LIMIT),
        name="gla_fwd" if forward else "gla_bwd",
    )(*args)


def _tail_kernel(x_ref, od_ref, og_ref, wo_ref, gf_ref, wg_ref, wu_ref, wd_ref, gl_ref, y_ref, *, ff_chunk):
    mix = (jnp.dot(od_ref[...], wo_ref[:DIL_WIDTH, :], preferred_element_type=F32)
           + jnp.dot(og_ref[...], wo_ref[DIL_WIDTH:, :], preferred_element_type=F32))
    x1 = x_ref[...] + mix
    h = _rms(x1, gf_ref[...]).astype(BF16)
    acc = x1
    for c0 in range(0, D_FF, ff_chunk):
        gate = jnp.dot(h, wg_ref[:, c0:c0 + ff_chunk], preferred_element_type=F32)
        up = jnp.dot(h, wu_ref[:, c0:c0 + ff_chunk], preferred_element_type=F32)
        act = (gate / (1.0 + jnp.exp(-gate)) * up).astype(BF16)
        acc = acc + jnp.dot(act, wd_ref[c0:c0 + ff_chunk, :], preferred_element_type=F32)
    y_ref[...] = _rms(acc, gl_ref[...])


def _tail(x, o_dil, o_gla, w_out, g_ffn, w_gate, w_up, w_down, g_final, tm, ff_chunk):
    N, D = x.shape
    tok = lambda w: pl.BlockSpec((tm, w), lambda i: (i, 0))
    full = lambda a: pl.BlockSpec(a.shape, lambda i: (0, 0), pipeline_mode=pl.Buffered(1))
    return pl.pallas_call(
        functools.partial(_tail_kernel, ff_chunk=ff_chunk),
        grid=(N // tm,),
        in_specs=[tok(D), tok(DIL_WIDTH), tok(GLA_WIDTH), full(w_out), full(g_ffn),
                  full(w_gate), full(w_up), full(w_down), full(g_final)],
        out_specs=tok(D),
        out_shape=jax.ShapeDtypeStruct((N, D), F32),
        compiler_params=pltpu.CompilerParams(
            dimension_semantics=("parallel",), vmem_limit_bytes=VMEM_LIMIT),
        name="tail",
    )(x, o_dil, o_gla, w_out, g_ffn, w_gate, w_up, w_down, g_final)


def _trunk(x, p, tm_in=512, ts_gla=512, tm_tail=512, ff_chunk=D_FF // 2):
    B, S, D = x.shape
    qkv, gq, gk, gv, gr, la = _inproj(x, p["g_mix"], p["w_in"], p["w_gate"], p["b_gate"], tm_in)
    o_dil = _dilated_mixture(qkv)
    ob = _gla_pass(gq, gk, la, gv, ts_gla, forward=False)
    o_gla = _gla_pass(gq, gk, la, gv, ts_gla, forward=True, ob=ob, gr=gr, g_gla=p["g_gla"])
    y = _tail(x.reshape(B * S, D), o_dil.reshape(B * S, DIL_WIDTH), o_gla.reshape(B * S, GLA_WIDTH),
              p["w_out"], p["g_ffn"], p["w_ffn_gate"], p["w_ffn_up"], p["w_ffn_down"], p["g_final"],
              tm_tail, ff_chunk)
    return y.reshape(B, S, D)


def _prepare_params(g_mix, w_in, w_gate_f, b_gate_f, w_gate_b, b_gate_b, g_gla, w_out,
                    g_ffn, w_ffn_gate, w_ffn_up, w_ffn_down, g_final):
    z = jnp.zeros_like(w_gate_f[0])
    w_gate = jnp.concatenate(
        [jnp.concatenate([w_gate_f[0], z], axis=1), jnp.concatenate([z, w_gate_b[0]], axis=1)], axis=0)
    return {
        "g_mix": g_mix[0][None, :],
        "w_in": w_in[0].astype(BF16),
        "w_gate": w_gate.astype(BF16),
        "b_gate": jnp.concatenate([b_gate_f[0], b_gate_b[0]])[None, :],
        "g_gla": g_gla[0][None, :],
        "w_out": w_out[0].astype(BF16),
        "g_ffn": g_ffn[0][None, :],
        "w_ffn_gate": w_ffn_gate[0].astype(BF16),
        "w_ffn_up": w_ffn_up[0].astype(BF16),
        "w_ffn_down": w_ffn_down[0].astype(BF16),
        "g_final": g_final[None, :],
    }


def kernel(x_prompt, x_sample, g_mix, w_in, w_gate_f, b_gate_f, w_gate_b, b_gate_b, g_gla, w_out,
           g_ffn, w_ffn_gate, w_ffn_up, w_ffn_down, g_final):
    p = _prepare_params(g_mix, w_in, w_gate_f, b_gate_f, w_gate_b, b_gate_b, g_gla, w_out,
                        g_ffn, w_ffn_gate, w_ffn_up, w_ffn_down, g_final)
    return (_trunk(x_prompt, p), _trunk(x_sample, p))
```

```python
import functools

import jax
import jax.numpy as jnp
from jax import lax
from jax.experimental import pallas as pl
from jax.experimental.pallas import tpu as pltpu

D_MODEL = 1024
HEAD_DIM = 64
N_HEADS_DIL = 8
DIL_WIDTH = N_HEADS_DIL * HEAD_DIM
N_HEADS_GLA = 4
GLA_DK = 64
GLA_DV = 128
GLA_KEY_WIDTH = N_HEADS_GLA * GLA_DK
GLA_WIDTH = N_HEADS_GLA * GLA_DV
GLA_RANK = 16
GLA_TAU = 16.0
GLA_CHUNK = 64
DILATED_PATTERNS = ((128, 1), (512, 4), (2048, 16))
ROT_DIM = HEAD_DIM // 4
ROPE_THETA = 500000.0
D_FF = 2816
EPS = 1e-6
NEG_INF = -1e30
PROJ_WIDTH = 3 * DIL_WIDTH + 2 * GLA_KEY_WIDTH + 2 * GLA_WIDTH + 2 * GLA_RANK
QKV_WIDTH = 3 * DIL_WIDTH

LANES = 128
BAND_HALF = 64
Q_TILE = 128
K_WINDOW = Q_TILE + 2 * BAND_HALF
DIL_STEP = 4
VMEM_LIMIT = 56 * 1024 * 1024

F32 = jnp.float32
BF16 = jnp.bfloat16


def _rms(x, g):
    return x * lax.rsqrt(jnp.mean(x * x, axis=-1, keepdims=True) + EPS) * g


def _inproj_kernel(x_ref, g_ref, w_ref, wg_ref, bg_ref, cos_ref, sa_ref, sb_ref,
                   nat_ref, d4_ref, d16_ref, gq_ref, gk_ref, gv_ref, gr_ref, la_ref,
                   slab, slab4):
    TM = x_ref.shape[1]
    R4 = TM // DIL_STEP
    R16 = R4 // DIL_STEP
    h = _rms(x_ref[0], g_ref[...]).astype(BF16)

    def proj(lo, hi):
        return jnp.dot(h, w_ref[:, lo:hi], preferred_element_type=F32)

    cos_t, sin_a, sin_b = cos_ref[...], sa_ref[...], sb_ref[...]

    def emit(yb, blk):
        cs = slice(blk * LANES, (blk + 1) * LANES)
        nat_ref[0, :, cs] = yb.astype(BF16)
        slab[blk] = yb
        for r in range(DIL_STEP):
            slab4[blk, r * R4:(r + 1) * R4, :] = slab[blk, pl.ds(r, R4, stride=DIL_STEP), :]
        for r in range(DIL_STEP):
            d4_ref[0, r, :, cs] = slab4[blk, r * R4:(r + 1) * R4, :].astype(BF16)
            for j in range(DIL_STEP):
                d16_ref[0, r + DIL_STEP * j, :, cs] = slab4[
                    blk, pl.ds(r * R4 + j, R16, stride=DIL_STEP), :].astype(BF16)

    def rotary(xb):
        return (xb * cos_t + pltpu.roll(xb, LANES - ROT_DIM // 2, 1) * sin_a
                + pltpu.roll(xb, ROT_DIM // 2, 1) * sin_b)

    nblk = DIL_WIDTH // LANES
    t = proj(0, DIL_WIDTH)
    for c in range(nblk):
        emit(rotary(t[:, c * LANES:(c + 1) * LANES]) * (HEAD_DIM ** -0.5), c)
    t = proj(DIL_WIDTH, 2 * DIL_WIDTH)
    for c in range(nblk):
        emit(rotary(t[:, c * LANES:(c + 1) * LANES]), nblk + c)
    t = proj(2 * DIL_WIDTH, 3 * DIL_WIDTH)
    for c in range(nblk):
        emit(t[:, c * LANES:(c + 1) * LANES], 2 * nblk + c)
    o = QKV_WIDTH
    gq_ref[0] = proj(o, o + GLA_KEY_WIDTH) * (GLA_DK ** -0.5)
    o += GLA_KEY_WIDTH
    gk_ref[0] = proj(o, o + GLA_KEY_WIDTH)
    o += GLA_KEY_WIDTH
    gv_ref[0] = proj(o, o + GLA_WIDTH).astype(BF16)
    o += GLA_WIDTH
    gr_ref[0] = proj(o, o + GLA_WIDTH)
    o += GLA_WIDTH
    g_lr = proj(o, o + 2 * GLA_RANK).astype(BF16)
    pre = jnp.dot(g_lr, wg_ref[...], preferred_element_type=F32) + bg_ref[...]
    la_ref[0] = (jnp.minimum(pre, 0.0) - jnp.log1p(jnp.exp(-jnp.abs(pre)))) * (1.0 / GLA_TAU)


def _rotary_tables(S):
    half = ROT_DIM // 2
    inv_freq = jnp.power(jnp.float32(ROPE_THETA), -jnp.arange(0, ROT_DIM, 2, dtype=F32) / ROT_DIM)
    ang = jnp.arange(S, dtype=F32)[:, None] * inv_freq[None, :]
    cos, sin = jnp.cos(ang), jnp.sin(ang)
    ones = jnp.ones((S, HEAD_DIM - ROT_DIM), F32)
    zeros = jnp.zeros((S, HEAD_DIM - ROT_DIM), F32)
    zh = jnp.zeros((S, half), F32)
    cos_t = jnp.concatenate([cos, cos, ones], axis=1)
    sin_a = jnp.concatenate([-sin, zh, zeros], axis=1)
    sin_b = jnp.concatenate([zh, sin, zeros], axis=1)
    rep = LANES // HEAD_DIM
    return tuple(jnp.tile(t, (1, rep)) for t in (cos_t, sin_a, sin_b))


def _inproj(x, g_mix, w_in, w_gate, b_gate, tm):
    B, S, D = x.shape
    d4, d16 = DIL_STEP, DIL_STEP * DIL_STEP
    cos_t, sin_a, sin_b = _rotary_tables(S)
    tok = lambda w: pl.BlockSpec((1, tm, w), lambda b, i: (b, i, 0))
    res = lambda d: pl.BlockSpec((1, d, tm // d, QKV_WIDTH), lambda b, i: (b, 0, i, 0))
    full = lambda a: pl.BlockSpec(a.shape, lambda b, i: (0,) * a.ndim)
    tab = pl.BlockSpec((tm, LANES), lambda b, i: (i, 0))
    out_shape = (
        jax.ShapeDtypeStruct((B, S, QKV_WIDTH), BF16),
        jax.ShapeDtypeStruct((B, d4, S // d4, QKV_WIDTH), BF16),
        jax.ShapeDtypeStruct((B, d16, S // d16, QKV_WIDTH), BF16),
        jax.ShapeDtypeStruct((B, S, GLA_KEY_WIDTH), F32),
        jax.ShapeDtypeStruct((B, S, GLA_KEY_WIDTH), F32),
        jax.ShapeDtypeStruct((B, S, GLA_WIDTH), BF16),
        jax.ShapeDtypeStruct((B, S, GLA_WIDTH), F32),
        jax.ShapeDtypeStruct((B, S, 2 * GLA_KEY_WIDTH), F32),
    )
    out_specs = (tok(QKV_WIDTH), res(d4), res(d16)) + tuple(tok(s.shape[-1]) for s in out_shape[3:])
    nslab = QKV_WIDTH // LANES
    return pl.pallas_call(
        _inproj_kernel,
        grid=(B, S // tm),
        in_specs=[tok(D), full(g_mix), full(w_in), full(w_gate), full(b_gate), tab, tab, tab],
        out_specs=out_specs,
        out_shape=out_shape,
        scratch_shapes=[pltpu.VMEM((nslab, tm, LANES), F32)] * 2,
        compiler_params=pltpu.CompilerParams(
            dimension_semantics=("parallel", "parallel"), vmem_limit_bytes=VMEM_LIMIT),
        name="inproj",
    )(x, g_mix, w_in, w_gate, b_gate, cos_t, sin_a, sin_b)


def _dil_kernel(*refs, L, pos_axis, has_prev, final):
    q_ref, kl_ref, km_ref, kr_ref, vl_ref, vm_ref, vr_ref = refs[:7]
    refs = refs[7:]
    if has_prev:
        op_ref, lp_ref = refs[:2]
        refs = refs[2:]
    if final:
        o_ref, kwin, vwin, so = refs
    else:
        o_ref, l_ref, kwin, vwin, so, sl = refs
    NSEQ, TL, CW = q_ref.shape
    i = pl.program_id(pos_axis)

    kwin[:, 0:BAND_HALF] = kl_ref[...]
    kwin[:, BAND_HALF:BAND_HALF + TL] = km_ref[...]
    kwin[:, BAND_HALF + TL:] = kr_ref[...]
    vwin[:, 0:BAND_HALF] = vl_ref[...]
    vwin[:, BAND_HALF:BAND_HALF + TL] = vm_ref[...]
    vwin[:, BAND_HALF + TL:] = vr_ref[...]

    lane = lax.broadcasted_iota(jnp.int32, (1, LANES), 1)
    head0 = lane < HEAD_DIM
    m0 = head0.astype(BF16)
    m1 = (1.0 - head0.astype(F32)).astype(BF16)
    row = lax.broadcasted_iota(jnp.int32, (Q_TILE, K_WINDOW), 0)
    col = lax.broadcasted_iota(jnp.int32, (Q_TILE, K_WINDOW), 1)
    band = (col >= row) & (col <= row + 2 * BAND_HALF)

    for j in range(TL // Q_TILE):
        r0 = j * Q_TILE
        kpos = i * TL + (r0 - BAND_HALF) + col
        valid = band & (kpos >= 0) & (kpos < L)
        bias = jnp.where(valid, 0.0, NEG_INF)
        bias2 = jnp.concatenate([bias, bias], axis=0)
        for n in range(NSEQ):
            for hp in range(CW // LANES):
                cs = slice(hp * LANES, (hp + 1) * LANES)
                q = q_ref[n, r0:r0 + Q_TILE, cs]
                k = kwin[n, r0:r0 + K_WINDOW, cs]
                v = vwin[n, r0:r0 + K_WINDOW, cs]
                qq = jnp.concatenate([q * m0, q * m1], axis=0)
                s = lax.dot_general(qq, k, (((1,), (1,)), ((), ())),
                                    preferred_element_type=F32) + bias2
                m = jnp.max(s, axis=1, keepdims=True)
                p = jnp.exp(s - m)
                l = jnp.sum(p, axis=1, keepdims=True)
                pv = jnp.dot(p.astype(BF16), v, preferred_element_type=F32)
                acc = jnp.where(head0, pv[:Q_TILE], pv[Q_TILE:])
                mm = jnp.where(head0, m[:Q_TILE], m[Q_TILE:])
                ll = jnp.where(head0, l[:Q_TILE], l[Q_TILE:])
                if has_prev:
                    op = op_ref[n, r0:r0 + Q_TILE, cs]
                    lp = lp_ref[n, r0:r0 + Q_TILE, cs]
                    mn = jnp.maximum(lp, mm)
                    w1 = jnp.exp(lp - mn)
                    w2 = jnp.exp(mm - mn)
                    den = w1 + ll * w2
                    o = (op * w1 + acc * w2) / den
                    lse = mn + jnp.log(den)
                else:
                    o = acc / ll
                    lse = mm + jnp.log(ll)
                if final:
                    so[hp, pl.ds(n + NSEQ * r0, Q_TILE, stride=NSEQ), :] = o
                else:
                    so[hp, r0:r0 + Q_TILE, :] = o
                    sl[hp, r0:r0 + Q_TILE, :] = lse

    for hp in range(CW // LANES):
        cs = slice(hp * LANES, (hp + 1) * LANES)
        if final:
            o_ref[:, cs] = so[hp].astype(o_ref.dtype)
        else:
            TQ = TL // DIL_STEP
            for r in range(DIL_STEP):
                o_ref[r, :, cs] = so[hp, pl.ds(r, TQ, stride=DIL_STEP), :]
                l_ref[r, :, cs] = sl[hp, pl.ds(r, TQ, stride=DIL_STEP), :]


def _dil_call(qkv, prev, L, grid, blk, in_idx, halo_l, halo_r, state_spec, out_specs, out_shape,
              pos_axis, final, name):
    NSEQ, TL, CW = blk
    main = lambda t: pl.BlockSpec((None, NSEQ, TL, CW), functools.partial(in_idx, t=t))
    left = lambda t: pl.BlockSpec((None, NSEQ, BAND_HALF, CW), functools.partial(halo_l, t=t))
    right = lambda t: pl.BlockSpec((None, NSEQ, BAND_HALF, CW), functools.partial(halo_r, t=t))
    in_specs = [main(0), left(1), main(1), right(1), left(2), main(2), right(2)]
    args = [qkv] * 7
    if prev is not None:
        in_specs += [state_spec, state_spec]
        args += list(prev)
    nhp = CW // LANES
    scratch = [pltpu.VMEM((NSEQ, TL + 2 * BAND_HALF, CW), BF16)] * 2
    scratch += [pltpu.VMEM((nhp, NSEQ * TL, LANES), F32)] * (1 if final else 2)
    return pl.pallas_call(
        functools.partial(_dil_kernel, L=L, pos_axis=pos_axis, has_prev=prev is not None, final=final),
        grid=grid,
        in_specs=in_specs,
        out_specs=out_specs,
        out_shape=out_shape,
        scratch_shapes=scratch,
        compiler_params=pltpu.CompilerParams(
            dimension_semantics=("parallel",) * len(grid), vmem_limit_bytes=VMEM_LIMIT),
        name=name,
    )(*args)


def _dilated_mixture(qkv_nat, qkv_d4, qkv_d16):
    B, S, _ = qkv_nat.shape
    W = DIL_WIDTH
    d4, d16 = DIL_STEP, DIL_STEP * DIL_STEP
    L4, L16 = S // d4, S // d16
    TL = 512
    nb = TL // BAND_HALF

    st4 = jax.ShapeDtypeStruct((B, d4, L4, W), F32)
    spec4 = pl.BlockSpec((None, d4, TL // d4, W), lambda b, i: (b, 0, i, 0))
    state = _dil_call(
        qkv_nat.reshape(B, 1, S, QKV_WIDTH), None, S, (B, S // TL), (1, TL, W),
        lambda b, i, t: (b, 0, i, t),
        lambda b, i, t: (b, 0, jnp.maximum(i * nb - 1, 0), t),
        lambda b, i, t: (b, 0, jnp.minimum((i + 1) * nb, S // BAND_HALF - 1), t),
        None, (spec4, spec4), (st4, st4), pos_axis=1, final=False, name="dilated_d1")

    st16 = jax.ShapeDtypeStruct((B, d4, d4, L16, W), F32)
    spec16 = pl.BlockSpec((None, d4, None, TL // d4, W), lambda b, r, i: (b, 0, r, i, 0))
    state = _dil_call(
        qkv_d4, state, L4, (B, d4, L4 // TL), (1, TL, W),
        lambda b, r, i, t: (b, r, i, t),
        lambda b, r, i, t: (b, r, jnp.maximum(i * nb - 1, 0), t),
        lambda b, r, i, t: (b, r, jnp.minimum((i + 1) * nb, L4 // BAND_HALF - 1), t),
        pl.BlockSpec((None, 1, TL, W), lambda b, r, i: (b, r, i, 0)),
        (spec16, spec16), (st16, st16), pos_axis=2, final=False, name="dilated_d4")
    state = tuple(s.reshape(B, d16, L16, W) for s in state)

    CW = 2 * LANES
    nbq = Q_TILE // BAND_HALF
    span = d16 * Q_TILE
    return _dil_call(
        qkv_d16, state, L16, (B, S // span, W // CW), (d16, Q_TILE, CW),
        lambda b, i, c, t: (b, 0, i, t * (W // CW) + c),
        lambda b, i, c, t: (b, 0, jnp.maximum(i * nbq - 1, 0), t * (W // CW) + c),
        lambda b, i, c, t: (b, 0, jnp.minimum((i + 1) * nbq, L16 // BAND_HALF - 1), t * (W // CW) + c),
        pl.BlockSpec((None, d16, Q_TILE, CW), lambda b, i, c: (b, 0, i, c)),
        pl.BlockSpec((None, span, CW), lambda b, i, c: (b, i, c)),
        jax.ShapeDtypeStruct((B, S, W), BF16), pos_axis=1, final=True, name="dilated_d16")


def _gla_kernel(*refs, TS, forward):
    if forward:
        q_ref, k_ref, la_ref, v_ref, ob_ref, r_ref, g_ref, o_ref, state_ref = refs
    else:
        q_ref, k_ref, la_ref, v_ref, o_ref, state_ref = refs
    C = GLA_CHUNK
    nch = TS // C

    @pl.when(pl.program_id(2) == 0)
    def _():
        state_ref[...] = jnp.zeros_like(state_ref)

    lane = lax.broadcasted_iota(jnp.int32, (1, LANES), 1)
    m0 = (lane < GLA_DK).astype(F32)
    m1 = 1.0 - m0
    tr = lax.broadcasted_iota(jnp.int32, (C, C), 0)
    tc = lax.broadcasted_iota(jnp.int32, (C, C), 1)
    if forward:
        cum = tc <= tr
        keep = tc <= tr
    else:
        cum = tc >= tr
        keep = tc > tr
    cum_m = cum.astype(BF16)
    keep2 = jnp.concatenate([keep, keep], axis=0)

    state_t = state_ref[...]
    order = range(nch) if forward else range(nch - 1, -1, -1)
    for c in order:
        rows = slice(c * C, (c + 1) * C)
        la = la_ref[0, rows, :]
        h1 = la.astype(BF16)
        r1 = la - h1.astype(F32)
        h2 = r1.astype(BF16)
        h3 = (r1 - h2.astype(F32)).astype(BF16)
        bb = jnp.dot(cum_m, jnp.concatenate([h1, h2, h3], axis=1), preferred_element_type=F32)
        b = bb[:, :LANES] + bb[:, LANES:2 * LANES] + bb[:, 2 * LANES:]
        total = b[C - 1:C, :] if forward else b[0:1, :]
        q = q_ref[0, rows, :]
        k = k_ref[0, rows, :]
        v = v_ref[0, rows, :]
        q_in = q * jnp.exp(b)
        k_in = (k * jnp.exp(-b)).astype(BF16)
        k_dec = k * jnp.exp(total - b)
        qs = jnp.concatenate([q_in * m0, q_in * m1], axis=0).astype(BF16)
        att = lax.dot_general(qs, k_in, (((1,), (1,)), ((), ())), preferred_element_type=F32)
        att = jnp.where(keep2, att, 0.0).astype(BF16)
        intra = jnp.dot(att, v, preferred_element_type=F32)
        inter = lax.dot_general(qs, state_t.astype(BF16), (((1,), (1,)), ((), ())),
                                preferred_element_type=F32)
        o = jnp.concatenate([intra[:C, :GLA_DV] + inter[:C], intra[C:, GLA_DV:] + inter[C:]], axis=1)
        ks = jnp.concatenate([k_dec * m0, k_dec * m1], axis=0).astype(BF16)
        vs = jnp.concatenate([v[:, :GLA_DV], v[:, GLA_DV:]], axis=0)
        kv_t = lax.dot_general(vs, ks, (((0,), (0,)), ((), ())), preferred_element_type=F32)
        state_t = state_t * jnp.exp(total) + kv_t
        if forward:
            tot = o + ob_ref[0, rows, :]
            r = r_ref[0, rows, :]
            gate = r / (1.0 + jnp.exp(-r))
            for hh in range(2):
                hs = slice(hh * GLA_DV, (hh + 1) * GLA_DV)
                o_ref[0, rows, hs] = (_rms(tot[:, hs], g_ref[...]) * gate[:, hs]).astype(o_ref.dtype)
        else:
            o_ref[0, rows, :] = o
    state_ref[...] = state_t


def _gla_pass(gq, gk, la, gv, ts, forward, ob=None, gr=None, g_gla=None):
    B, S, _ = gq.shape
    nt = S // ts
    tile = (lambda i: i) if forward else (lambda i: nt - 1 - i)
    la_off = 0 if forward else GLA_KEY_WIDTH // LANES
    key = pl.BlockSpec((1, ts, LANES), lambda b, h, i: (b, tile(i), h))
    gate = pl.BlockSpec((1, ts, LANES), lambda b, h, i: (b, tile(i), la_off + h))
    val = pl.BlockSpec((1, ts, 2 * GLA_DV), lambda b, h, i: (b, tile(i), h))
    in_specs = [key, key, gate, val]
    args = [gq, gk, la, gv]
    if forward:
        in_specs += [val, val, pl.BlockSpec(g_gla.shape, lambda b, h, i: (0, 0))]
        args += [ob, gr, g_gla]
    return pl.pallas_call(
        functools.partial(_gla_kernel, TS=ts, forward=forward),
        grid=(B, N_HEADS_GLA // 2, nt),
        in_specs=in_specs,
        out_specs=val,
        out_shape=jax.ShapeDtypeStruct((B, S, GLA_WIDTH), BF16 if forward else F32),
        scratch_shapes=[pltpu.VMEM((GLA_DV, 2 * GLA_DK), F32)],
        compiler_params=pltpu.CompilerParams(
            dimension_semantics=("parallel", "parallel", "arbitrary"), vmem_limit_bytes=VMEM_LIMIT),
        name="gla_fwd" if forward else "gla_bwd",
    )(*args)


def _tail_kernel(x_ref, od_ref, og_ref, wo_ref, gf_ref, wg_ref, wu_ref, wd_ref, gl_ref, y_ref, *, ff_chunk):
    mix = (jnp.dot(od_ref[...], wo_ref[:DIL_WIDTH, :], preferred_element_type=F32)
           + jnp.dot(og_ref[...], wo_ref[DIL_WIDTH:, :], preferred_element_type=F32))
    x1 = x_ref[...] + mix
    h = _rms(x1, gf_ref[...]).astype(BF16)
    acc = x1
    for c0 in range(0, D_FF, ff_chunk):
        gate = jnp.dot(h, wg_ref[:, c0:c0 + ff_chunk], preferred_element_type=F32)
        up = jnp.dot(h, wu_ref[:, c0:c0 + ff_chunk], preferred_element_type=F32)
        act = (gate / (1.0 + jnp.exp(-gate)) * up).astype(BF16)
        acc = acc + jnp.dot(act, wd_ref[c0:c0 + ff_chunk, :], preferred_element_type=F32)
    y_ref[...] = _rms(acc, gl_ref[...])


def _tail(x, o_dil, o_gla, w_out, g_ffn, w_gate, w_up, w_down, g_final, tm, ff_chunk):
    N, D = x.shape
    tok = lambda w: pl.BlockSpec((tm, w), lambda i: (i, 0))
    full = lambda a: pl.BlockSpec(a.shape, lambda i: (0, 0), pipeline_mode=pl.Buffered(1))
    return pl.pallas_call(
        functools.partial(_tail_kernel, ff_chunk=ff_chunk),
        grid=(N // tm,),
        in_specs=[tok(D), tok(DIL_WIDTH), tok(GLA_WIDTH), full(w_out), full(g_ffn),
                  full(w_gate), full(w_up), full(w_down), full(g_final)],
        out_specs=tok(D),
        out_shape=jax.ShapeDtypeStruct((N, D), F32),
        compiler_params=pltpu.CompilerParams(
            dimension_semantics=("parallel",), vmem_limit_bytes=VMEM_LIMIT),
        name="tail",
    )(x, o_dil, o_gla, w_out, g_ffn, w_gate, w_up, w_down, g_final)


def _trunk(x, p, tm_in=512, ts_gla=512, tm_tail=512, ff_chunk=D_FF // 2):
    B, S, D = x.shape
    qkv_nat, qkv_d4, qkv_d16, gq, gk, gv, gr, la = _inproj(
        x, p["g_mix"], p["w_in"], p["w_gate"], p["b_gate"], tm_in)
    o_dil = _dilated_mixture(qkv_nat, qkv_d4, qkv_d16)
    ob = _gla_pass(gq, gk, la, gv, ts_gla, forward=False)
    o_gla = _gla_pass(gq, gk, la, gv, ts_gla, forward=True, ob=ob, gr=gr, g_gla=p["g_gla"])
    y = _tail(x.reshape(B * S, D), o_dil.reshape(B * S, DIL_WIDTH), o_gla.reshape(B * S, GLA_WIDTH),
              p["w_out"], p["g_ffn"], p["w_ffn_gate"], p["w_ffn_up"], p["w_ffn_down"], p["g_final"],
              tm_tail, ff_chunk)
    return y.reshape(B, S, D)


def _prepare_params(g_mix, w_in, w_gate_f, b_gate_f, w_gate_b, b_gate_b, g_gla, w_out,
                    g_ffn, w_ffn_gate, w_ffn_up, w_ffn_down, g_final):
    z = jnp.zeros_like(w_gate_f[0])
    w_gate = jnp.concatenate(
        [jnp.concatenate([w_gate_f[0], z], axis=1), jnp.concatenate([z, w_gate_b[0]], axis=1)], axis=0)
    return {
        "g_mix": g_mix[0][None, :],
        "w_in": w_in[0].astype(BF16),
        "w_gate": w_gate.astype(BF16),
        "b_gate": jnp.concatenate([b_gate_f[0], b_gate_b[0]])[None, :],
        "g_gla": g_gla[0][None, :],
        "w_out": w_out[0].astype(BF16),
        "g_ffn": g_ffn[0][None, :],
        "w_ffn_gate": w_ffn_gate[0].astype(BF16),
        "w_ffn_up": w_ffn_up[0].astype(BF16),
        "w_ffn_down": w_ffn_down[0].astype(BF16),
        "g_final": g_final[None, :],
    }


def kernel(x_prompt, x_sample, g_mix, w_in, w_gate_f, b_gate_f, w_gate_b, b_gate_b, g_gla, w_out,
           g_ffn, w_ffn_gate, w_ffn_up, w_ffn_down, g_final):
    p = _prepare_params(g_mix, w_in, w_gate_f, b_gate_f, w_gate_b, b_gate_b, g_gla, w_out,
                        g_ffn, w_ffn_gate, w_ffn_up, w_ffn_down, g_final)
    return (_trunk(x_prompt, p), _trunk(x_sample, p))
```

```python
import functools

import jax
import jax.numpy as jnp
from jax import lax
from jax.experimental import pallas as pl
from jax.experimental.pallas import tpu as pltpu

D_MODEL = 1024
HEAD_DIM = 64
N_HEADS_DIL = 8
DIL_WIDTH = N_HEADS_DIL * HEAD_DIM
N_HEADS_GLA = 4
GLA_DK = 64
GLA_DV = 128
GLA_KEY_WIDTH = N_HEADS_GLA * GLA_DK
GLA_WIDTH = N_HEADS_GLA * GLA_DV
GLA_RANK = 16
GLA_TAU = 16.0
GLA_CHUNK = 64
DILATED_PATTERNS = ((128, 1), (512, 4), (2048, 16))
ROT_DIM = HEAD_DIM // 4
ROPE_THETA = 500000.0
D_FF = 2816
EPS = 1e-6
NEG_INF = -1e30
PROJ_WIDTH = 3 * DIL_WIDTH + 2 * GLA_KEY_WIDTH + 2 * GLA_WIDTH + 2 * GLA_RANK
QKV_WIDTH = 3 * DIL_WIDTH

LANES = 128
BAND_HALF = 64
Q_TILE = 128
K_WINDOW = Q_TILE + 2 * BAND_HALF
DIL_STEP = 4
VMEM_LIMIT = 56 * 1024 * 1024

F32 = jnp.float32
BF16 = jnp.bfloat16


def _rms(x, g):
    return x * lax.rsqrt(jnp.mean(x * x, axis=-1, keepdims=True) + EPS) * g


def _inproj_kernel(x_ref, g_ref, w_ref, wg_ref, bg_ref, cos_ref, sa_ref, sb_ref,
                   nat_ref, d4_ref, d16_ref, gq_ref, gk_ref, gv_ref, gr_ref, la_ref,
                   slab, slab4):
    TM = x_ref.shape[1]
    R4 = TM // DIL_STEP
    R16 = R4 // DIL_STEP
    h = _rms(x_ref[0], g_ref[...]).astype(BF16)

    def proj(lo, hi):
        return jnp.dot(h, w_ref[:, lo:hi], preferred_element_type=F32)

    cos_t, sin_a, sin_b = cos_ref[...], sa_ref[...], sb_ref[...]

    def emit(yb, blk):
        cs = slice(blk * LANES, (blk + 1) * LANES)
        nat_ref[0, :, cs] = yb.astype(BF16)
        slab[blk] = yb
        for r in range(DIL_STEP):
            slab4[blk, r * R4:(r + 1) * R4, :] = slab[blk, pl.ds(r, R4, stride=DIL_STEP), :]
        for r in range(DIL_STEP):
            d4_ref[0, r, :, cs] = slab4[blk, r * R4:(r + 1) * R4, :].astype(BF16)
            for j in range(DIL_STEP):
                d16_ref[0, r + DIL_STEP * j, :, cs] = slab4[
                    blk, pl.ds(r * R4 + j, R16, stride=DIL_STEP), :].astype(BF16)

    def rotary(xb):
        return (xb * cos_t + pltpu.roll(xb, LANES - ROT_DIM // 2, 1) * sin_a
                + pltpu.roll(xb, ROT_DIM // 2, 1) * sin_b)

    nblk = DIL_WIDTH // LANES
    t = proj(0, DIL_WIDTH)
    for c in range(nblk):
        emit(rotary(t[:, c * LANES:(c + 1) * LANES]) * (HEAD_DIM ** -0.5), c)
    t = proj(DIL_WIDTH, 2 * DIL_WIDTH)
    for c in range(nblk):
        emit(rotary(t[:, c * LANES:(c + 1) * LANES]), nblk + c)
    t = proj(2 * DIL_WIDTH, 3 * DIL_WIDTH)
    for c in range(nblk):
        emit(t[:, c * LANES:(c + 1) * LANES], 2 * nblk + c)
    o = QKV_WIDTH
    gq_ref[0] = proj(o, o + GLA_KEY_WIDTH) * (GLA_DK ** -0.5)
    o += GLA_KEY_WIDTH
    gk_ref[0] = proj(o, o + GLA_KEY_WIDTH)
    o += GLA_KEY_WIDTH
    gv_ref[0] = proj(o, o + GLA_WIDTH).astype(BF16)
    o += GLA_WIDTH
    gr_ref[0] = proj(o, o + GLA_WIDTH)
    o += GLA_WIDTH
    g_lr = proj(o, o + 2 * GLA_RANK).astype(BF16)
    pre = jnp.dot(g_lr, wg_ref[...], preferred_element_type=F32) + bg_ref[...]
    la_ref[0] = (jnp.minimum(pre, 0.0) - jnp.log1p(jnp.exp(-jnp.abs(pre)))) * (1.0 / GLA_TAU)


def _rotary_tables(S):
    half = ROT_DIM // 2
    inv_freq = jnp.power(jnp.float32(ROPE_THETA), -jnp.arange(0, ROT_DIM, 2, dtype=F32) / ROT_DIM)
    ang = jnp.arange(S, dtype=F32)[:, None] * inv_freq[None, :]
    cos, sin = jnp.cos(ang), jnp.sin(ang)
    ones = jnp.ones((S, HEAD_DIM - ROT_DIM), F32)
    zeros = jnp.zeros((S, HEAD_DIM - ROT_DIM), F32)
    zh = jnp.zeros((S, half), F32)
    cos_t = jnp.concatenate([cos, cos, ones], axis=1)
    sin_a = jnp.concatenate([-sin, zh, zeros], axis=1)
    sin_b = jnp.concatenate([zh, sin, zeros], axis=1)
    rep = LANES // HEAD_DIM
    return tuple(jnp.tile(t, (1, rep)) for t in (cos_t, sin_a, sin_b))


def _inproj(x, g_mix, w_in, w_gate, b_gate, tm):
    B, S, D = x.shape
    d4, d16 = DIL_STEP, DIL_STEP * DIL_STEP
    cos_t, sin_a, sin_b = _rotary_tables(S)
    tok = lambda w: pl.BlockSpec((1, tm, w), lambda b, i: (b, i, 0))
    res = lambda d: pl.BlockSpec((1, d, tm // d, QKV_WIDTH), lambda b, i: (b, 0, i, 0))
    full = lambda a: pl.BlockSpec(a.shape, lambda b, i: (0,) * a.ndim)
    tab = pl.BlockSpec((tm, LANES), lambda b, i: (i, 0))
    out_shape = (
        jax.ShapeDtypeStruct((B, S, QKV_WIDTH), BF16),
        jax.ShapeDtypeStruct((B, d4, S // d4, QKV_WIDTH), BF16),
        jax.ShapeDtypeStruct((B, d16, S // d16, QKV_WIDTH), BF16),
        jax.ShapeDtypeStruct((B, S, GLA_KEY_WIDTH), F32),
        jax.ShapeDtypeStruct((B, S, GLA_KEY_WIDTH), F32),
        jax.ShapeDtypeStruct((B, S, GLA_WIDTH), BF16),
        jax.ShapeDtypeStruct((B, S, GLA_WIDTH), F32),
        jax.ShapeDtypeStruct((B, S, 2 * GLA_KEY_WIDTH), F32),
    )
    out_specs = (tok(QKV_WIDTH), res(d4), res(d16)) + tuple(tok(s.shape[-1]) for s in out_shape[3:])
    nslab = QKV_WIDTH // LANES
    return pl.pallas_call(
        _inproj_kernel,
        grid=(B, S // tm),
        in_specs=[tok(D), full(g_mix), full(w_in), full(w_gate), full(b_gate), tab, tab, tab],
        out_specs=out_specs,
        out_shape=out_shape,
        scratch_shapes=[pltpu.VMEM((nslab, tm, LANES), F32)] * 2,
        compiler_params=pltpu.CompilerParams(
            dimension_semantics=("parallel", "parallel"), vmem_limit_bytes=VMEM_LIMIT),
        name="inproj",
    )(x, g_mix, w_in, w_gate, b_gate, cos_t, sin_a, sin_b)


def _dil_kernel(*refs, L, pos_axis, has_prev, final):
    q_ref, kl_ref, km_ref, kr_ref, vl_ref, vm_ref, vr_ref = refs[:7]
    refs = refs[7:]
    if has_prev:
        op_ref, lp_ref = refs[:2]
        refs = refs[2:]
    if final:
        o_ref, kwin, vwin, so = refs
    else:
        o_ref, l_ref, kwin, vwin, so, sl = refs
    NSEQ, TL, CW = q_ref.shape
    i = pl.program_id(pos_axis)

    kwin[:, 0:BAND_HALF] = kl_ref[...]
    kwin[:, BAND_HALF:BAND_HALF + TL] = km_ref[...]
    kwin[:, BAND_HALF + TL:] = kr_ref[...]
    vwin[:, 0:BAND_HALF] = vl_ref[...]
    vwin[:, BAND_HALF:BAND_HALF + TL] = vm_ref[...]
    vwin[:, BAND_HALF + TL:] = vr_ref[...]

    lane = lax.broadcasted_iota(jnp.int32, (1, LANES), 1)
    head0 = lane < HEAD_DIM
    m0 = head0.astype(BF16)
    m1 = (1.0 - head0.astype(F32)).astype(BF16)
    row = lax.broadcasted_iota(jnp.int32, (Q_TILE, K_WINDOW), 0)
    col = lax.broadcasted_iota(jnp.int32, (Q_TILE, K_WINDOW), 1)
    band = (col >= row) & (col <= row + 2 * BAND_HALF)

    for j in range(TL // Q_TILE):
        r0 = j * Q_TILE
        kpos = i * TL + (r0 - BAND_HALF) + col
        valid = band & (kpos >= 0) & (kpos < L)
        bias = jnp.where(valid, 0.0, NEG_INF)
        bias2 = jnp.concatenate([bias, bias], axis=0)
        for n in range(NSEQ):
            for hp in range(CW // LANES):
                cs = slice(hp * LANES, (hp + 1) * LANES)
                q = q_ref[n, r0:r0 + Q_TILE, cs]
                k = kwin[n, r0:r0 + K_WINDOW, cs]
                v = vwin[n, r0:r0 + K_WINDOW, cs]
                qq = jnp.concatenate([q * m0, q * m1], axis=0)
                s = lax.dot_general(qq, k, (((1,), (1,)), ((), ())),
                                    preferred_element_type=F32) + bias2
                m = jnp.max(s, axis=1, keepdims=True)
                p = jnp.exp(s - m)
                l = jnp.sum(p, axis=1, keepdims=True)
                pv = jnp.dot(p.astype(BF16), v, preferred_element_type=F32)
                acc = jnp.where(head0, pv[:Q_TILE], pv[Q_TILE:])
                mm = jnp.where(head0, m[:Q_TILE], m[Q_TILE:])
                ll = jnp.where(head0, l[:Q_TILE], l[Q_TILE:])
                if has_prev:
                    op = op_ref[n, r0:r0 + Q_TILE, cs]
                    lp = lp_ref[n, r0:r0 + Q_TILE, cs]
                    mn = jnp.maximum(lp, mm)
                    w1 = jnp.exp(lp - mn)
                    w2 = jnp.exp(mm - mn)
                    den = w1 + ll * w2
                    o = (op * w1 + acc * w2) / den
                    lse = mn + jnp.log(den)
                else:
                    o = acc / ll
                    lse = mm + jnp.log(ll)
                if final:
                    so[hp, pl.ds(n + NSEQ * r0, Q_TILE, stride=NSEQ), :] = o
                else:
                    so[hp, r0:r0 + Q_TILE, :] = o
                    sl[hp, r0:r0 + Q_TILE, :] = lse

    for hp in range(CW // LANES):
        cs = slice(hp * LANES, (hp + 1) * LANES)
        if final:
            o_ref[:, cs] = so[hp].astype(o_ref.dtype)
        else:
            TQ = TL // DIL_STEP
            for r in range(DIL_STEP):
                o_ref[r, :, cs] = so[hp, pl.ds(r, TQ, stride=DIL_STEP), :]
                l_ref[r, :, cs] = sl[hp, pl.ds(r, TQ, stride=DIL_STEP), :]


def _dil_call(qkv, prev, L, grid, blk, in_idx, halo_l, halo_r, state_spec, out_specs, out_shape,
              pos_axis, final, name):
    NSEQ, TL, CW = blk
    main = lambda t: pl.BlockSpec((None, NSEQ, TL, CW), functools.partial(in_idx, t=t))
    left = lambda t: pl.BlockSpec((None, NSEQ, BAND_HALF, CW), functools.partial(halo_l, t=t))
    right = lambda t: pl.BlockSpec((None, NSEQ, BAND_HALF, CW), functools.partial(halo_r, t=t))
    in_specs = [main(0), left(1), main(1), right(1), left(2), main(2), right(2)]
    args = [qkv] * 7
    if prev is not None:
        in_specs += [state_spec, state_spec]
        args += list(prev)
    nhp = CW // LANES
    scratch = [pltpu.VMEM((NSEQ, TL + 2 * BAND_HALF, CW), BF16)] * 2
    scratch += [pltpu.VMEM((nhp, NSEQ * TL, LANES), F32)] * (1 if final else 2)
    return pl.pallas_call(
        functools.partial(_dil_kernel, L=L, pos_axis=pos_axis, has_prev=prev is not None, final=final),
        grid=grid,
        in_specs=in_specs,
        out_specs=out_specs,
        out_shape=out_shape,
        scratch_shapes=scratch,
        compiler_params=pltpu.CompilerParams(
            dimension_semantics=("parallel",) * len(grid), vmem_limit_bytes=VMEM_LIMIT),
        name=name,
    )(*args)


def _dilated_mixture(qkv_nat, qkv_d4, qkv_d16):
    B, S, _ = qkv_nat.shape
    W = DIL_WIDTH
    d4, d16 = DIL_STEP, DIL_STEP * DIL_STEP
    L4, L16 = S // d4, S // d16
    TL = 512
    nb = TL // BAND_HALF

    st4 = jax.ShapeDtypeStruct((B, d4, L4, W), F32)
    spec4 = pl.BlockSpec((None, d4, TL // d4, W), lambda b, i: (b, 0, i, 0))
    state = _dil_call(
        qkv_nat.reshape(B, 1, S, QKV_WIDTH), None, S, (B, S // TL), (1, TL, W),
        lambda b, i, t: (b, 0, i, t),
        lambda b, i, t: (b, 0, jnp.maximum(i * nb - 1, 0), t),
        lambda b, i, t: (b, 0, jnp.minimum((i + 1) * nb, S // BAND_HALF - 1), t),
        None, (spec4, spec4), (st4, st4), pos_axis=1, final=False, name="dilated_d1")

    st16 = jax.ShapeDtypeStruct((B, d4, d4, L16, W), F32)
    spec16 = pl.BlockSpec((None, d4, None, TL // d4, W), lambda b, r, i: (b, 0, r, i, 0))
    state = _dil_call(
        qkv_d4, state, L4, (B, d4, L4 // TL), (1, TL, W),
        lambda b, r, i, t: (b, r, i, t),
        lambda b, r, i, t: (b, r, jnp.maximum(i * nb - 1, 0), t),
        lambda b, r, i, t: (b, r, jnp.minimum((i + 1) * nb, L4 // BAND_HALF - 1), t),
        pl.BlockSpec((None, 1, TL, W), lambda b, r, i: (b, r, i, 0)),
        (spec16, spec16), (st16, st16), pos_axis=2, final=False, name="dilated_d4")
    state = tuple(s.reshape(B, d16, L16, W) for s in state)

    CW = 2 * LANES
    nbq = Q_TILE // BAND_HALF
    span = d16 * Q_TILE
    return _dil_call(
        qkv_d16, state, L16, (B, S // span, W // CW), (d16, Q_TILE, CW),
        lambda b, i, c, t: (b, 0, i, t * (W // CW) + c),
        lambda b, i, c, t: (b, 0, jnp.maximum(i * nbq - 1, 0), t * (W // CW) + c),
        lambda b, i, c, t: (b, 0, jnp.minimum((i + 1) * nbq, L16 // BAND_HALF - 1), t * (W // CW) + c),
        pl.BlockSpec((None, d16, Q_TILE, CW), lambda b, i, c: (b, 0, i, c)),
        pl.BlockSpec((None, span, CW), lambda b, i, c: (b, i, c)),
        jax.ShapeDtypeStruct((B, S, W), BF16), pos_axis=1, final=True, name="dilated_d16")


def _gla_kernel(*refs, TS, forward):
    if forward:
        q_ref, k_ref, la_ref, v_ref, ob_ref, r_ref, g_ref, o_ref, state_ref = refs
    else:
        q_ref, k_ref, la_ref, v_ref, o_ref, state_ref = refs
    C = GLA_CHUNK
    nch = TS // C

    @pl.when(pl.program_id(2) == 0)
    def _():
        state_ref[...] = jnp.zeros_like(state_ref)

    lane = lax.broadcasted_iota(jnp.int32, (1, LANES), 1)
    m0 = (lane < GLA_DK).astype(F32)
    m1 = 1.0 - m0
    tr = lax.broadcasted_iota(jnp.int32, (C, C), 0)
    tc = lax.broadcasted_iota(jnp.int32, (C, C), 1)
    keep = (tc <= tr) if forward else (tc > tr)
    keep2 = jnp.concatenate([keep, keep], axis=0)[None]

    b = la_ref[0]
    rowc = lax.broadcasted_iota(jnp.int32, (TS, LANES), 0) % C
    step = 1
    while step < C:
        if forward:
            b = b + jnp.where(rowc >= step, pltpu.roll(b, step, 0), 0.0)
        else:
            b = b + jnp.where(rowc < C - step, pltpu.roll(b, TS - step, 0), 0.0)
        step *= 2
    b = b.reshape(nch, C, LANES)
    total = b[:, C - 1:C, :] if forward else b[:, 0:1, :]

    q = q_ref[0].reshape(nch, C, LANES)
    k = k_ref[0].reshape(nch, C, LANES)
    v = v_ref[0].reshape(nch, C, 2 * GLA_DV)
    q_in = q * jnp.exp(b)
    k_in = (k * jnp.exp(-b)).astype(BF16)
    k_dec = k * jnp.exp(total - b)
    qs = jnp.concatenate([q_in * m0, q_in * m1], axis=1).astype(BF16)
    att = jnp.einsum("cqd,ckd->cqk", qs, k_in, preferred_element_type=F32)
    att = jnp.where(keep2, att, 0.0).astype(BF16)
    intra = jnp.einsum("cqk,ckv->cqv", att, v, preferred_element_type=F32)
    ks = jnp.concatenate([k_dec * m0, k_dec * m1], axis=1).astype(BF16)
    vs = jnp.concatenate([v[:, :, :GLA_DV], v[:, :, GLA_DV:]], axis=1)
    kv_t = jnp.einsum("ctv,ctl->cvl", vs, ks, preferred_element_type=F32)

    decay = jnp.exp(total)
    state_t = state_ref[...]
    states = [None] * nch
    for c in (range(nch) if forward else range(nch - 1, -1, -1)):
        states[c] = state_t.astype(BF16)
        state_t = state_t * decay[c] + kv_t[c]
    state_ref[...] = state_t
    inter = jnp.einsum("cql,cvl->cqv", qs, jnp.stack(states), preferred_element_type=F32)
    o = jnp.concatenate([intra[:, :C, :GLA_DV] + inter[:, :C], intra[:, C:, GLA_DV:] + inter[:, C:]],
                        axis=2).reshape(TS, 2 * GLA_DV)
    if forward:
        tot = o + ob_ref[0]
        r = r_ref[0]
        gate = r / (1.0 + jnp.exp(-r))
        for hh in range(2):
            hs = slice(hh * GLA_DV, (hh + 1) * GLA_DV)
            o_ref[0, :, hs] = (_rms(tot[:, hs], g_ref[...]) * gate[:, hs]).astype(o_ref.dtype)
    else:
        o_ref[0] = o


def _gla_pass(gq, gk, la, gv, ts, forward, ob=None, gr=None, g_gla=None):
    B, S, _ = gq.shape
    nt = S // ts
    tile = (lambda i: i) if forward else (lambda i: nt - 1 - i)
    la_off = 0 if forward else GLA_KEY_WIDTH // LANES
    key = pl.BlockSpec((1, ts, LANES), lambda b, h, i: (b, tile(i), h))
    gate = pl.BlockSpec((1, ts, LANES), lambda b, h, i: (b, tile(i), la_off + h))
    val = pl.BlockSpec((1, ts, 2 * GLA_DV), lambda b, h, i: (b, tile(i), h))
    in_specs = [key, key, gate, val]
    args = [gq, gk, la, gv]
    if forward:
        in_specs += [val, val, pl.BlockSpec(g_gla.shape, lambda b, h, i: (0, 0))]
        args += [ob, gr, g_gla]
    return pl.pallas_call(
        functools.partial(_gla_kernel, TS=ts, forward=forward),
        grid=(B, N_HEADS_GLA // 2, nt),
        in_specs=in_specs,
        out_specs=val,
        out_shape=jax.ShapeDtypeStruct((B, S, GLA_WIDTH), BF16 if forward else F32),
        scratch_shapes=[pltpu.VMEM((GLA_DV, 2 * GLA_DK), F32)],
        compiler_params=pltpu.CompilerParams(
            dimension_semantics=("parallel", "parallel", "arbitrary"), vmem_limit_bytes=VMEM_LIMIT),
        name="gla_fwd" if forward else "gla_bwd",
    )(*args)


def _tail_kernel(x_ref, od_ref, og_ref, wo_ref, gf_ref, wg_ref, wu_ref, wd_ref, gl_ref, y_ref, *, ff_chunk):
    mix = (jnp.dot(od_ref[...], wo_ref[:DIL_WIDTH, :], preferred_element_type=F32)
           + jnp.dot(og_ref[...], wo_ref[DIL_WIDTH:, :], preferred_element_type=F32))
    x1 = x_ref[...] + mix
    h = _rms(x1, gf_ref[...]).astype(BF16)
    acc = x1
    for c0 in range(0, D_FF, ff_chunk):
        gate = jnp.dot(h, wg_ref[:, c0:c0 + ff_chunk], preferred_element_type=F32)
        up = jnp.dot(h, wu_ref[:, c0:c0 + ff_chunk], preferred_element_type=F32)
        act = (gate / (1.0 + jnp.exp(-gate)) * up).astype(BF16)
        acc = acc + jnp.dot(act, wd_ref[c0:c0 + ff_chunk, :], preferred_element_type=F32)
    y_ref[...] = _rms(acc, gl_ref[...])


def _tail(x, o_dil, o_gla, w_out, g_ffn, w_gate, w_up, w_down, g_final, tm, ff_chunk):
    N, D = x.shape
    tok = lambda w: pl.BlockSpec((tm, w), lambda i: (i, 0))
    full = lambda a: pl.BlockSpec(a.shape, lambda i: (0, 0), pipeline_mode=pl.Buffered(1))
    return pl.pallas_call(
        functools.partial(_tail_kernel, ff_chunk=ff_chunk),
        grid=(N // tm,),
        in_specs=[tok(D), tok(DIL_WIDTH), tok(GLA_WIDTH), full(w_out), full(g_ffn),
                  full(w_gate), full(w_up), full(w_down), full(g_final)],
        out_specs=tok(D),
        out_shape=jax.ShapeDtypeStruct((N, D), F32),
        compiler_params=pltpu.CompilerParams(
            dimension_semantics=("parallel",), vmem_limit_bytes=VMEM_LIMIT),
        name="tail",
    )(x, o_dil, o_gla, w_out, g_ffn, w_gate, w_up, w_down, g_final)


def _trunk(x, p, tm_in=512, ts_gla=1024, tm_tail=512, ff_chunk=D_FF // 2):
    B, S, D = x.shape
    qkv_nat, qkv_d4, qkv_d16, gq, gk, gv, gr, la = _inproj(
        x, p["g_mix"], p["w_in"], p["w_gate"], p["b_gate"], tm_in)
    o_dil = _dilated_mixture(qkv_nat, qkv_d4, qkv_d16)
    ob = _gla_pass(gq, gk, la, gv, ts_gla, forward=False)
    o_gla = _gla_pass(gq, gk, la, gv, ts_gla, forward=True, ob=ob, gr=gr, g_gla=p["g_gla"])
    y = _tail(x.reshape(B * S, D), o_dil.reshape(B * S, DIL_WIDTH), o_gla.reshape(B * S, GLA_WIDTH),
              p["w_out"], p["g_ffn"], p["w_ffn_gate"], p["w_ffn_up"], p["w_ffn_down"], p["g_final"],
              tm_tail, ff_chunk)
    return y.reshape(B, S, D)


def _prepare_params(g_mix, w_in, w_gate_f, b_gate_f, w_gate_b, b_gate_b, g_gla, w_out,
                    g_ffn, w_ffn_gate, w_ffn_up, w_ffn_down, g_final):
    z = jnp.zeros_like(w_gate_f[0])
    w_gate = jnp.concatenate(
        [jnp.concatenate([w_gate_f[0], z], axis=1), jnp.concatenate([z, w_gate_b[0]], axis=1)], axis=0)
    return {
        "g_mix": g_mix[0][None, :],
        "w_in": w_in[0].astype(BF16),
        "w_gate": w_gate.astype(BF16),
        "b_gate": jnp.concatenate([b_gate_f[0], b_gate_b[0]])[None, :],
        "g_gla": g_gla[0][None, :],
        "w_out": w_out[0].astype(BF16),
        "g_ffn": g_ffn[0][None, :],
        "w_ffn_gate": w_ffn_gate[0].astype(BF16),
        "w_ffn_up": w_ffn_up[0].astype(BF16),
        "w_ffn_down": w_ffn_down[0].astype(BF16),
        "g_final": g_final[None, :],
    }


def kernel(x_prompt, x_sample, g_mix, w_in, w_gate_f, b_gate_f, w_gate_b, b_gate_b, g_gla, w_out,
           g_ffn, w_ffn_gate, w_ffn_up, w_ffn_down, g_final):
    p = _prepare_params(g_mix, w_in, w_gate_f, b_gate_f, w_gate_b, b_gate_b, g_gla, w_out,
                        g_ffn, w_ffn_gate, w_ffn_up, w_ffn_down, g_final)
    return (_trunk(x_prompt, p), _trunk(x_sample, p))
```

```python
import functools

import jax
import jax.numpy as jnp
from jax import lax
from jax.experimental import pallas as pl
from jax.experimental.pallas import tpu as pltpu

D_MODEL = 1024
HEAD_DIM = 64
N_HEADS_DIL = 8
DIL_WIDTH = N_HEADS_DIL * HEAD_DIM
N_HEADS_GLA = 4
GLA_DK = 64
GLA_DV = 128
GLA_KEY_WIDTH = N_HEADS_GLA * GLA_DK
GLA_WIDTH = N_HEADS_GLA * GLA_DV
GLA_RANK = 16
GLA_TAU = 16.0
GLA_CHUNK = 64
DILATED_PATTERNS = ((128, 1), (512, 4), (2048, 16))
ROT_DIM = HEAD_DIM // 4
ROPE_THETA = 500000.0
D_FF = 2816
EPS = 1e-6
NEG_INF = -1e30
LOG2_E = 1.4426950408889634
PROJ_WIDTH = 3 * DIL_WIDTH + 2 * GLA_KEY_WIDTH + 2 * GLA_WIDTH + 2 * GLA_RANK
QKV_WIDTH = 3 * DIL_WIDTH

LANES = 128
BAND_HALF = 64
Q_TILE = 128
K_WINDOW = Q_TILE + 2 * BAND_HALF
DIL_STEP = 4
VMEM_LIMIT = 56 * 1024 * 1024

F32 = jnp.float32
BF16 = jnp.bfloat16


def _rms(x, g):
    return x * lax.rsqrt(jnp.mean(x * x, axis=-1, keepdims=True) + EPS) * g


def _inproj_kernel(x_ref, g_ref, w_ref, wg_ref, bg_ref, cos_ref, sa_ref, sb_ref,
                   nat_ref, d4_ref, d16_ref, gq_ref, gk_ref, gv_ref, gr_ref, la_ref,
                   slab, slab4):
    TM = x_ref.shape[1]
    R4 = TM // DIL_STEP
    R16 = R4 // DIL_STEP
    h = _rms(x_ref[0], g_ref[...]).astype(BF16)

    def proj(lo, hi):
        return jnp.dot(h, w_ref[:, lo:hi], preferred_element_type=F32)

    cos_t, sin_a, sin_b = cos_ref[...], sa_ref[...], sb_ref[...]

    def emit(yb, blk):
        cs = slice(blk * LANES, (blk + 1) * LANES)
        nat_ref[0, :, cs] = yb.astype(BF16)
        slab[blk] = yb
        for r in range(DIL_STEP):
            slab4[blk, r * R4:(r + 1) * R4, :] = slab[blk, pl.ds(r, R4, stride=DIL_STEP), :]
        for r in range(DIL_STEP):
            d4_ref[0, r, :, cs] = slab4[blk, r * R4:(r + 1) * R4, :].astype(BF16)
            for j in range(DIL_STEP):
                d16_ref[0, r + DIL_STEP * j, :, cs] = slab4[
                    blk, pl.ds(r * R4 + j, R16, stride=DIL_STEP), :].astype(BF16)

    def rotary(xb):
        return (xb * cos_t + pltpu.roll(xb, LANES - ROT_DIM // 2, 1) * sin_a
                + pltpu.roll(xb, ROT_DIM // 2, 1) * sin_b)

    nblk = DIL_WIDTH // LANES
    t = proj(0, DIL_WIDTH)
    for c in range(nblk):
        emit(rotary(t[:, c * LANES:(c + 1) * LANES]) * (HEAD_DIM ** -0.5 * LOG2_E), c)
    t = proj(DIL_WIDTH, 2 * DIL_WIDTH)
    for c in range(nblk):
        emit(rotary(t[:, c * LANES:(c + 1) * LANES]), nblk + c)
    t = proj(2 * DIL_WIDTH, 3 * DIL_WIDTH)
    for c in range(nblk):
        emit(t[:, c * LANES:(c + 1) * LANES], 2 * nblk + c)
    o = QKV_WIDTH
    gq_ref[0] = proj(o, o + GLA_KEY_WIDTH) * (GLA_DK ** -0.5)
    o += GLA_KEY_WIDTH
    gk_ref[0] = proj(o, o + GLA_KEY_WIDTH)
    o += GLA_KEY_WIDTH
    gv_ref[0] = proj(o, o + GLA_WIDTH).astype(BF16)
    o += GLA_WIDTH
    gr_ref[0] = proj(o, o + GLA_WIDTH)
    o += GLA_WIDTH
    g_lr = proj(o, o + 2 * GLA_RANK).astype(BF16)
    pre = jnp.dot(g_lr, wg_ref[...], preferred_element_type=F32) + bg_ref[...]
    la_ref[0] = (jnp.minimum(pre, 0.0) - jnp.log(1.0 + jnp.exp(-jnp.abs(pre)))) * (1.0 / GLA_TAU)


def _rotary_tables(S):
    half = ROT_DIM // 2
    inv_freq = jnp.power(jnp.float32(ROPE_THETA), -jnp.arange(0, ROT_DIM, 2, dtype=F32) / ROT_DIM)
    ang = jnp.arange(S, dtype=F32)[:, None] * inv_freq[None, :]
    cos, sin = jnp.cos(ang), jnp.sin(ang)
    ones = jnp.ones((S, HEAD_DIM - ROT_DIM), F32)
    zeros = jnp.zeros((S, HEAD_DIM - ROT_DIM), F32)
    zh = jnp.zeros((S, half), F32)
    cos_t = jnp.concatenate([cos, cos, ones], axis=1)
    sin_a = jnp.concatenate([-sin, zh, zeros], axis=1)
    sin_b = jnp.concatenate([zh, sin, zeros], axis=1)
    rep = LANES // HEAD_DIM
    return tuple(jnp.tile(t, (1, rep)) for t in (cos_t, sin_a, sin_b))


def _inproj(x, g_mix, w_in, w_gate, b_gate, tm):
    B, S, D = x.shape
    d4, d16 = DIL_STEP, DIL_STEP * DIL_STEP
    cos_t, sin_a, sin_b = _rotary_tables(S)
    tok = lambda w: pl.BlockSpec((1, tm, w), lambda b, i: (b, i, 0))
    res = lambda d: pl.BlockSpec((1, d, tm // d, QKV_WIDTH), lambda b, i: (b, 0, i, 0))
    full = lambda a: pl.BlockSpec(a.shape, lambda b, i: (0,) * a.ndim)
    tab = pl.BlockSpec((tm, LANES), lambda b, i: (i, 0))
    out_shape = (
        jax.ShapeDtypeStruct((B, S, QKV_WIDTH), BF16),
        jax.ShapeDtypeStruct((B, d4, S // d4, QKV_WIDTH), BF16),
        jax.ShapeDtypeStruct((B, d16, S // d16, QKV_WIDTH), BF16),
        jax.ShapeDtypeStruct((B, S, GLA_KEY_WIDTH), F32),
        jax.ShapeDtypeStruct((B, S, GLA_KEY_WIDTH), F32),
        jax.ShapeDtypeStruct((B, S, GLA_WIDTH), BF16),
        jax.ShapeDtypeStruct((B, S, GLA_WIDTH), F32),
        jax.ShapeDtypeStruct((B, S, 2 * GLA_KEY_WIDTH), F32),
    )
    out_specs = (tok(QKV_WIDTH), res(d4), res(d16)) + tuple(tok(s.shape[-1]) for s in out_shape[3:])
    nslab = QKV_WIDTH // LANES
    return pl.pallas_call(
        _inproj_kernel,
        grid=(B, S // tm),
        in_specs=[tok(D), full(g_mix), full(w_in), full(w_gate), full(b_gate), tab, tab, tab],
        out_specs=out_specs,
        out_shape=out_shape,
        scratch_shapes=[pltpu.VMEM((nslab, tm, LANES), F32)] * 2,
        compiler_params=pltpu.CompilerParams(
            dimension_semantics=("parallel", "parallel"), vmem_limit_bytes=VMEM_LIMIT),
        name="inproj",
    )(x, g_mix, w_in, w_gate, b_gate, cos_t, sin_a, sin_b)


def _dil_kernel(*refs, L, pos_axis, has_prev, final):
    q_ref, kl_ref, km_ref, kr_ref, vl_ref, vm_ref, vr_ref = refs[:7]
    refs = refs[7:]
    if has_prev:
        op_ref, lp_ref = refs[:2]
        refs = refs[2:]
    if final:
        o_ref, kwin, vwin, so = refs
    else:
        o_ref, l_ref, kwin, vwin, so, sl = refs
    NSEQ, TL, CW = q_ref.shape
    i = pl.program_id(pos_axis)

    kwin[:, 0:BAND_HALF] = kl_ref[...]
    kwin[:, BAND_HALF:BAND_HALF + TL] = km_ref[...]
    kwin[:, BAND_HALF + TL:] = kr_ref[...]
    vwin[:, 0:BAND_HALF] = vl_ref[...]
    vwin[:, BAND_HALF:BAND_HALF + TL] = vm_ref[...]
    vwin[:, BAND_HALF + TL:] = vr_ref[...]

    lane = lax.broadcasted_iota(jnp.int32, (1, LANES), 1)
    head0 = lane < HEAD_DIM
    m0 = head0.astype(BF16)
    m1 = (1.0 - head0.astype(F32)).astype(BF16)
    row = lax.broadcasted_iota(jnp.int32, (Q_TILE, K_WINDOW), 0)
    col = lax.broadcasted_iota(jnp.int32, (Q_TILE, K_WINDOW), 1)
    band = (col >= row) & (col <= row + 2 * BAND_HALF)
    ones_v = jnp.ones((K_WINDOW, LANES), BF16)

    for j in range(TL // Q_TILE):
        r0 = j * Q_TILE
        kpos = i * TL + (r0 - BAND_HALF) + col
        valid = band & (kpos >= 0) & (kpos < L)
        bias = jnp.where(valid, 0.0, NEG_INF)
        bias2 = jnp.concatenate([bias, bias], axis=0)
        for n in range(NSEQ):
            for hp in range(CW // LANES):
                cs = slice(hp * LANES, (hp + 1) * LANES)
                q = q_ref[n, r0:r0 + Q_TILE, cs]
                k = kwin[n, r0:r0 + K_WINDOW, cs]
                v = vwin[n, r0:r0 + K_WINDOW, cs]
                qq = jnp.concatenate([q * m0, q * m1], axis=0)
                s = lax.dot_general(qq, k, (((1,), (1,)), ((), ())),
                                    preferred_element_type=F32) + bias2
                m = jnp.max(s, axis=1, keepdims=True)
                p = jnp.exp2(s - m).astype(BF16)
                pv = jnp.dot(p, jnp.concatenate([v, ones_v], axis=1),
                             preferred_element_type=F32)
                acc = jnp.where(head0, pv[:Q_TILE, :LANES], pv[Q_TILE:, :LANES])
                ll = jnp.where(head0, pv[:Q_TILE, LANES:], pv[Q_TILE:, LANES:])
                mm = jnp.where(head0, m[:Q_TILE], m[Q_TILE:])
                if has_prev:
                    op = op_ref[n, r0:r0 + Q_TILE, cs]
                    lp = lp_ref[n, r0:r0 + Q_TILE, cs]
                    mn = jnp.maximum(lp, mm)
                    w1 = jnp.exp2(lp - mn)
                    w2 = jnp.exp2(mm - mn)
                    den = w1 + ll * w2
                    o = (op * w1 + acc * w2) / den
                    lse = mn + jnp.log2(den)
                else:
                    o = acc / ll
                    lse = mm + jnp.log2(ll)
                if final:
                    so[hp, pl.ds(n + NSEQ * r0, Q_TILE, stride=NSEQ), :] = o
                else:
                    so[hp, r0:r0 + Q_TILE, :] = o
                    sl[hp, r0:r0 + Q_TILE, :] = lse

    for hp in range(CW // LANES):
        cs = slice(hp * LANES, (hp + 1) * LANES)
        if final:
            o_ref[:, cs] = so[hp].astype(o_ref.dtype)
        else:
            TQ = TL // DIL_STEP
            for r in range(DIL_STEP):
                o_ref[r, :, cs] = so[hp, pl.ds(r, TQ, stride=DIL_STEP), :]
                l_ref[r, :, cs] = sl[hp, pl.ds(r, TQ, stride=DIL_STEP), :]


def _dil_call(qkv, prev, L, grid, blk, in_idx, halo_l, halo_r, state_spec, out_specs, out_shape,
              pos_axis, final, name):
    NSEQ, TL, CW = blk
    main = lambda t: pl.BlockSpec((None, NSEQ, TL, CW), functools.partial(in_idx, t=t))
    left = lambda t: pl.BlockSpec((None, NSEQ, BAND_HALF, CW), functools.partial(halo_l, t=t))
    right = lambda t: pl.BlockSpec((None, NSEQ, BAND_HALF, CW), functools.partial(halo_r, t=t))
    in_specs = [main(0), left(1), main(1), right(1), left(2), main(2), right(2)]
    args = [qkv] * 7
    if prev is not None:
        in_specs += [state_spec, state_spec]
        args += list(prev)
    nhp = CW // LANES
    scratch = [pltpu.VMEM((NSEQ, TL + 2 * BAND_HALF, CW), BF16)] * 2
    scratch += [pltpu.VMEM((nhp, NSEQ * TL, LANES), F32)] * (1 if final else 2)
    return pl.pallas_call(
        functools.partial(_dil_kernel, L=L, pos_axis=pos_axis, has_prev=prev is not None, final=final),
        grid=grid,
        in_specs=in_specs,
        out_specs=out_specs,
        out_shape=out_shape,
        scratch_shapes=scratch,
        compiler_params=pltpu.CompilerParams(
            dimension_semantics=("parallel",) * len(grid), vmem_limit_bytes=VMEM_LIMIT),
        name=name,
    )(*args)


def _dilated_mixture(qkv_nat, qkv_d4, qkv_d16):
    B, S, _ = qkv_nat.shape
    W = DIL_WIDTH
    d4, d16 = DIL_STEP, DIL_STEP * DIL_STEP
    L4, L16 = S // d4, S // d16
    TL = 512
    nb = TL // BAND_HALF

    st4 = jax.ShapeDtypeStruct((B, d4, L4, W), F32)
    spec4 = pl.BlockSpec((None, d4, TL // d4, W), lambda b, i: (b, 0, i, 0))
    state = _dil_call(
        qkv_nat.reshape(B, 1, S, QKV_WIDTH), None, S, (B, S // TL), (1, TL, W),
        lambda b, i, t: (b, 0, i, t),
        lambda b, i, t: (b, 0, jnp.maximum(i * nb - 1, 0), t),
        lambda b, i, t: (b, 0, jnp.minimum((i + 1) * nb, S // BAND_HALF - 1), t),
        None, (spec4, spec4), (st4, st4), pos_axis=1, final=False, name="dilated_d1")

    st16 = jax.ShapeDtypeStruct((B, d4, d4, L16, W), F32)
    spec16 = pl.BlockSpec((None, d4, None, TL // d4, W), lambda b, r, i: (b, 0, r, i, 0))
    state = _dil_call(
        qkv_d4, state, L4, (B, d4, L4 // TL), (1, TL, W),
        lambda b, r, i, t: (b, r, i, t),
        lambda b, r, i, t: (b, r, jnp.maximum(i * nb - 1, 0), t),
        lambda b, r, i, t: (b, r, jnp.minimum((i + 1) * nb, L4 // BAND_HALF - 1), t),
        pl.BlockSpec((None, 1, TL, W), lambda b, r, i: (b, r, i, 0)),
        (spec16, spec16), (st16, st16), pos_axis=2, final=False, name="dilated_d4")
    state = tuple(s.reshape(B, d16, L16, W) for s in state)

    CW = 2 * LANES
    nbq = Q_TILE // BAND_HALF
    span = d16 * Q_TILE
    return _dil_call(
        qkv_d16, state, L16, (B, S // span, W // CW), (d16, Q_TILE, CW),
        lambda b, i, c, t: (b, 0, i, t * (W // CW) + c),
        lambda b, i, c, t: (b, 0, jnp.maximum(i * nbq - 1, 0), t * (W // CW) + c),
        lambda b, i, c, t: (b, 0, jnp.minimum((i + 1) * nbq, L16 // BAND_HALF - 1), t * (W // CW) + c),
        pl.BlockSpec((None, d16, Q_TILE, CW), lambda b, i, c: (b, 0, i, c)),
        pl.BlockSpec((None, span, CW), lambda b, i, c: (b, i, c)),
        jax.ShapeDtypeStruct((B, S, W), BF16), pos_axis=1, final=True, name="dilated_d16")


def _gla_kernel(*refs, TS, forward):
    if forward:
        q_ref, k_ref, la_ref, v_ref, ob_ref, r_ref, g_ref, o_ref, state_ref = refs
    else:
        q_ref, k_ref, la_ref, v_ref, o_ref, state_ref = refs
    C = GLA_CHUNK
    nch = TS // C

    @pl.when(pl.program_id(2) == 0)
    def _():
        state_ref[...] = jnp.zeros_like(state_ref)

    lane = lax.broadcasted_iota(jnp.int32, (1, LANES), 1)
    m0 = (lane < GLA_DK).astype(F32)
    m1 = 1.0 - m0
    tr = lax.broadcasted_iota(jnp.int32, (C, C), 0)
    tc = lax.broadcasted_iota(jnp.int32, (C, C), 1)
    keep = (tc <= tr) if forward else (tc > tr)
    keep2 = jnp.concatenate([keep, keep], axis=0)[None]

    b = la_ref[0]
    rowc = lax.broadcasted_iota(jnp.int32, (TS, LANES), 0) % C
    step = 1
    while step < C:
        if forward:
            b = b + jnp.where(rowc >= step, pltpu.roll(b, step, 0), 0.0)
        else:
            b = b + jnp.where(rowc < C - step, pltpu.roll(b, TS - step, 0), 0.0)
        step *= 2
    b = b.reshape(nch, C, LANES)
    total = b[:, C - 1:C, :] if forward else b[:, 0:1, :]

    q = q_ref[0].reshape(nch, C, LANES)
    k = k_ref[0].reshape(nch, C, LANES)
    v = v_ref[0].reshape(nch, C, 2 * GLA_DV)
    q_in = q * jnp.exp(b)
    k_in = (k * jnp.exp(-b)).astype(BF16)
    k_dec = k * jnp.exp(total - b)
    qs = jnp.concatenate([q_in * m0, q_in * m1], axis=1).astype(BF16)
    att = jnp.einsum("cqd,ckd->cqk", qs, k_in, preferred_element_type=F32)
    att = jnp.where(keep2, att, 0.0).astype(BF16)
    intra = jnp.einsum("cqk,ckv->cqv", att, v, preferred_element_type=F32)
    ks = jnp.concatenate([k_dec * m0, k_dec * m1], axis=1).astype(BF16)
    vs = jnp.concatenate([v[:, :, :GLA_DV], v[:, :, GLA_DV:]], axis=1)
    kv_t = jnp.einsum("ctv,ctl->cvl", vs, ks, preferred_element_type=F32)

    decay = jnp.exp(total)
    state_t = state_ref[...]
    states = [None] * nch
    for c in (range(nch) if forward else range(nch - 1, -1, -1)):
        states[c] = state_t.astype(BF16)
        state_t = state_t * decay[c] + kv_t[c]
    state_ref[...] = state_t
    inter = jnp.einsum("cql,cvl->cqv", qs, jnp.stack(states), preferred_element_type=F32)
    o = jnp.concatenate([intra[:, :C, :GLA_DV] + inter[:, :C], intra[:, C:, GLA_DV:] + inter[:, C:]],
                        axis=2).reshape(TS, 2 * GLA_DV)
    if forward:
        tot = o + ob_ref[0]
        r = r_ref[0]
        gate = r / (1.0 + jnp.exp(-r))
        for hh in range(2):
            hs = slice(hh * GLA_DV, (hh + 1) * GLA_DV)
            o_ref[0, :, hs] = (_rms(tot[:, hs], g_ref[...]) * gate[:, hs]).astype(o_ref.dtype)
    else:
        o_ref[0] = o


def _gla_pass(gq, gk, la, gv, ts, forward, ob=None, gr=None, g_gla=None):
    B, S, _ = gq.shape
    nt = S // ts
    tile = (lambda i: i) if forward else (lambda i: nt - 1 - i)
    la_off = 0 if forward else GLA_KEY_WIDTH // LANES
    key = pl.BlockSpec((1, ts, LANES), lambda b, h, i: (b, tile(i), h))
    gate = pl.BlockSpec((1, ts, LANES), lambda b, h, i: (b, tile(i), la_off + h))
    val = pl.BlockSpec((1, ts, 2 * GLA_DV), lambda b, h, i: (b, tile(i), h))
    in_specs = [key, key, gate, val]
    args = [gq, gk, la, gv]
    if forward:
        in_specs += [val, val, pl.BlockSpec(g_gla.shape, lambda b, h, i: (0, 0))]
        args += [ob, gr, g_gla]
    return pl.pallas_call(
        functools.partial(_gla_kernel, TS=ts, forward=forward),
        grid=(B, N_HEADS_GLA // 2, nt),
        in_specs=in_specs,
        out_specs=val,
        out_shape=jax.ShapeDtypeStruct((B, S, GLA_WIDTH), BF16 if forward else F32),
        scratch_shapes=[pltpu.VMEM((GLA_DV, 2 * GLA_DK), F32)],
        compiler_params=pltpu.CompilerParams(
            dimension_semantics=("parallel", "parallel", "arbitrary"), vmem_limit_bytes=VMEM_LIMIT),
        name="gla_fwd" if forward else "gla_bwd",
    )(*args)


def _tail_kernel(x_ref, od_ref, og_ref, wo_ref, gf_ref, wg_ref, wu_ref, wd_ref, gl_ref, y_ref, *, ff_chunk):
    mix = (jnp.dot(od_ref[...], wo_ref[:DIL_WIDTH, :], preferred_element_type=F32)
           + jnp.dot(og_ref[...], wo_ref[DIL_WIDTH:, :], preferred_element_type=F32))
    x1 = x_ref[...] + mix
    h = _rms(x1, gf_ref[...]).astype(BF16)
    acc = x1
    for c0 in range(0, D_FF, ff_chunk):
        gate = jnp.dot(h, wg_ref[:, c0:c0 + ff_chunk], preferred_element_type=F32)
        up = jnp.dot(h, wu_ref[:, c0:c0 + ff_chunk], preferred_element_type=F32)
        act = (gate / (1.0 + jnp.exp(-gate)) * up).astype(BF16)
        acc = acc + jnp.dot(act, wd_ref[c0:c0 + ff_chunk, :], preferred_element_type=F32)
    y_ref[...] = _rms(acc, gl_ref[...])


def _tail(x, o_dil, o_gla, w_out, g_ffn, w_gate, w_up, w_down, g_final, tm, ff_chunk):
    N, D = x.shape
    tok = lambda w: pl.BlockSpec((tm, w), lambda i: (i, 0))
    full = lambda a: pl.BlockSpec(a.shape, lambda i: (0, 0), pipeline_mode=pl.Buffered(1))
    return pl.pallas_call(
        functools.partial(_tail_kernel, ff_chunk=ff_chunk),
        grid=(N // tm,),
        in_specs=[tok(D), tok(DIL_WIDTH), tok(GLA_WIDTH), full(w_out), full(g_ffn),
                  full(w_gate), full(w_up), full(w_down), full(g_final)],
        out_specs=tok(D),
        out_shape=jax.ShapeDtypeStruct((N, D), F32),
        compiler_params=pltpu.CompilerParams(
            dimension_semantics=("parallel",), vmem_limit_bytes=VMEM_LIMIT),
        name="tail",
    )(x, o_dil, o_gla, w_out, g_ffn, w_gate, w_up, w_down, g_final)


def _trunk(x, p, tm_in=512, ts_gla=1024, tm_tail=512, ff_chunk=D_FF // 2):
    B, S, D = x.shape
    qkv_nat, qkv_d4, qkv_d16, gq, gk, gv, gr, la = _inproj(
        x, p["g_mix"], p["w_in"], p["w_gate"], p["b_gate"], tm_in)
    o_dil = _dilated_mixture(qkv_nat, qkv_d4, qkv_d16)
    ob = _gla_pass(gq, gk, la, gv, ts_gla, forward=False)
    o_gla = _gla_pass(gq, gk, la, gv, ts_gla, forward=True, ob=ob, gr=gr, g_gla=p["g_gla"])
    y = _tail(x.reshape(B * S, D), o_dil.reshape(B * S, DIL_WIDTH), o_gla.reshape(B * S, GLA_WIDTH),
              p["w_out"], p["g_ffn"], p["w_ffn_gate"], p["w_ffn_up"], p["w_ffn_down"], p["g_final"],
              tm_tail, ff_chunk)
    return y.reshape(B, S, D)


def _prepare_params(g_mix, w_in, w_gate_f, b_gate_f, w_gate_b, b_gate_b, g_gla, w_out,
                    g_ffn, w_ffn_gate, w_ffn_up, w_ffn_down, g_final):
    z = jnp.zeros_like(w_gate_f[0])
    w_gate = jnp.concatenate(
        [jnp.concatenate([w_gate_f[0], z], axis=1), jnp.concatenate([z, w_gate_b[0]], axis=1)], axis=0)
    return {
        "g_mix": g_mix[0][None, :],
        "w_in": w_in[0].astype(BF16),
        "w_gate": w_gate.astype(BF16),
        "b_gate": jnp.concatenate([b_gate_f[0], b_gate_b[0]])[None, :],
        "g_gla": g_gla[0][None, :],
        "w_out": w_out[0].astype(BF16),
        "g_ffn": g_ffn[0][None, :],
        "w_ffn_gate": w_ffn_gate[0].astype(BF16),
        "w_ffn_up": w_ffn_up[0].astype(BF16),
        "w_ffn_down": w_ffn_down[0].astype(BF16),
        "g_final": g_final[None, :],
    }


def kernel(x_prompt, x_sample, g_mix, w_in, w_gate_f, b_gate_f, w_gate_b, b_gate_b, g_gla, w_out,
           g_ffn, w_ffn_gate, w_ffn_up, w_ffn_down, g_final):
    p = _prepare_params(g_mix, w_in, w_gate_f, b_gate_f, w_gate_b, b_gate_b, g_gla, w_out,
                        g_ffn, w_ffn_gate, w_ffn_up, w_ffn_down, g_final)
    return (_trunk(x_prompt, p), _trunk(x_sample, p))
```

```python
import functools

import jax
import jax.numpy as jnp
from jax import lax
from jax.experimental import pallas as pl
from jax.experimental.pallas import tpu as pltpu

D_MODEL = 1024
HEAD_DIM = 64
N_HEADS_DIL = 8
DIL_WIDTH = N_HEADS_DIL * HEAD_DIM
N_HEADS_GLA = 4
GLA_DK = 64
GLA_DV = 128
GLA_KEY_WIDTH = N_HEADS_GLA * GLA_DK
GLA_WIDTH = N_HEADS_GLA * GLA_DV
GLA_RANK = 16
GLA_TAU = 16.0
GLA_CHUNK = 64
DILATED_PATTERNS = ((128, 1), (512, 4), (2048, 16))
ROT_DIM = HEAD_DIM // 4
ROPE_THETA = 500000.0
D_FF = 2816
EPS = 1e-6
NEG_INF = -1e30
LOG2_E = 1.4426950408889634
PROJ_WIDTH = 3 * DIL_WIDTH + 2 * GLA_KEY_WIDTH + 2 * GLA_WIDTH + 2 * GLA_RANK
QKV_WIDTH = 3 * DIL_WIDTH

LANES = 128
BAND_HALF = 64
Q_TILE = 128
K_WINDOW = Q_TILE + 2 * BAND_HALF
DIL_STEP = 4
VMEM_LIMIT = 56 * 1024 * 1024

F32 = jnp.float32
BF16 = jnp.bfloat16


def _rms(x, g):
    return x * lax.rsqrt(jnp.mean(x * x, axis=-1, keepdims=True) + EPS) * g


def _inproj_kernel(x_ref, g_ref, w_ref, wg_ref, bg_ref, cos_ref, sa_ref, sb_ref,
                   nat_ref, d4_ref, d16_ref, gq_ref, gk_ref, gv_ref, gr_ref, la_ref,
                   slab, slab4):
    TM = x_ref.shape[1]
    R4 = TM // DIL_STEP
    R16 = R4 // DIL_STEP
    h = _rms(x_ref[0], g_ref[...]).astype(BF16)

    def proj(lo, hi):
        return jnp.dot(h, w_ref[:, lo:hi], preferred_element_type=F32)

    cos_t, sin_a, sin_b = cos_ref[...], sa_ref[...], sb_ref[...]

    def emit(yb, blk):
        cs = slice(blk * LANES, (blk + 1) * LANES)
        nat_ref[0, :, cs] = yb.astype(BF16)
        slab[blk] = yb
        for r in range(DIL_STEP):
            slab4[blk, r * R4:(r + 1) * R4, :] = slab[blk, pl.ds(r, R4, stride=DIL_STEP), :]
        for r in range(DIL_STEP):
            d4_ref[0, r, :, cs] = slab4[blk, r * R4:(r + 1) * R4, :].astype(BF16)
            for j in range(DIL_STEP):
                d16_ref[0, r + DIL_STEP * j, :, cs] = slab4[
                    blk, pl.ds(r * R4 + j, R16, stride=DIL_STEP), :].astype(BF16)

    def rotary(xb):
        return (xb * cos_t + pltpu.roll(xb, LANES - ROT_DIM // 2, 1) * sin_a
                + pltpu.roll(xb, ROT_DIM // 2, 1) * sin_b)

    o = PROJ_WIDTH - 2 * GLA_RANK
    g_lr = proj(o, o + 2 * GLA_RANK).astype(BF16)
    pre = jnp.dot(g_lr, wg_ref[...], preferred_element_type=F32) + bg_ref[...]
    la_ref[0] = (jnp.minimum(pre, 0.0) - jnp.log(1.0 + jnp.exp(-jnp.abs(pre)))) * (1.0 / GLA_TAU)

    nblk = DIL_WIDTH // LANES
    t = proj(0, DIL_WIDTH)
    for c in range(nblk):
        emit(rotary(t[:, c * LANES:(c + 1) * LANES]) * (HEAD_DIM ** -0.5 * LOG2_E), c)
    t = proj(DIL_WIDTH, 2 * DIL_WIDTH)
    for c in range(nblk):
        emit(rotary(t[:, c * LANES:(c + 1) * LANES]), nblk + c)
    t = proj(2 * DIL_WIDTH, 3 * DIL_WIDTH)
    for c in range(nblk):
        emit(t[:, c * LANES:(c + 1) * LANES], 2 * nblk + c)
    o = QKV_WIDTH
    gq_ref[0] = (proj(o, o + GLA_KEY_WIDTH) * (GLA_DK ** -0.5)).astype(BF16)
    o += GLA_KEY_WIDTH
    gk_ref[0] = proj(o, o + GLA_KEY_WIDTH).astype(BF16)
    o += GLA_KEY_WIDTH
    gv_ref[0] = proj(o, o + GLA_WIDTH).astype(BF16)
    o += GLA_WIDTH
    gr_ref[0] = proj(o, o + GLA_WIDTH).astype(BF16)


def _rotary_tables(S):
    half = ROT_DIM // 2
    inv_freq = jnp.power(jnp.float32(ROPE_THETA), -jnp.arange(0, ROT_DIM, 2, dtype=F32) / ROT_DIM)
    ang = jnp.arange(S, dtype=F32)[:, None] * inv_freq[None, :]
    cos, sin = jnp.cos(ang), jnp.sin(ang)
    ones = jnp.ones((S, HEAD_DIM - ROT_DIM), F32)
    zeros = jnp.zeros((S, HEAD_DIM - ROT_DIM), F32)
    zh = jnp.zeros((S, half), F32)
    cos_t = jnp.concatenate([cos, cos, ones], axis=1)
    sin_a = jnp.concatenate([-sin, zh, zeros], axis=1)
    sin_b = jnp.concatenate([zh, sin, zeros], axis=1)
    rep = LANES // HEAD_DIM
    return tuple(jnp.tile(t, (1, rep)) for t in (cos_t, sin_a, sin_b))


def _inproj(x, g_mix, w_in, w_gate, b_gate, tm):
    B, S, D = x.shape
    d4, d16 = DIL_STEP, DIL_STEP * DIL_STEP
    cos_t, sin_a, sin_b = _rotary_tables(S)
    tok = lambda w: pl.BlockSpec((1, tm, w), lambda b, i: (b, i, 0))
    res = lambda d: pl.BlockSpec((1, d, tm // d, QKV_WIDTH), lambda b, i: (b, 0, i, 0))
    full = lambda a: pl.BlockSpec(a.shape, lambda b, i: (0,) * a.ndim)
    tab = pl.BlockSpec((tm, LANES), lambda b, i: (i, 0))
    out_shape = (
        jax.ShapeDtypeStruct((B, S, QKV_WIDTH), BF16),
        jax.ShapeDtypeStruct((B, d4, S // d4, QKV_WIDTH), BF16),
        jax.ShapeDtypeStruct((B, d16, S // d16, QKV_WIDTH), BF16),
        jax.ShapeDtypeStruct((B, S, GLA_KEY_WIDTH), BF16),
        jax.ShapeDtypeStruct((B, S, GLA_KEY_WIDTH), BF16),
        jax.ShapeDtypeStruct((B, S, GLA_WIDTH), BF16),
        jax.ShapeDtypeStruct((B, S, GLA_WIDTH), BF16),
        jax.ShapeDtypeStruct((B, S, 2 * GLA_KEY_WIDTH), F32),
    )
    out_specs = (tok(QKV_WIDTH), res(d4), res(d16)) + tuple(tok(s.shape[-1]) for s in out_shape[3:])
    nslab = QKV_WIDTH // LANES
    return pl.pallas_call(
        _inproj_kernel,
        grid=(B, S // tm),
        in_specs=[tok(D), full(g_mix), full(w_in), full(w_gate), full(b_gate), tab, tab, tab],
        out_specs=out_specs,
        out_shape=out_shape,
        scratch_shapes=[pltpu.VMEM((nslab, tm, LANES), F32)] * 2,
        compiler_params=pltpu.CompilerParams(
            dimension_semantics=("parallel", "parallel"), vmem_limit_bytes=VMEM_LIMIT),
        name="inproj",
    )(x, g_mix, w_in, w_gate, b_gate, cos_t, sin_a, sin_b)


def _dil_kernel(*refs, L, pos_axis, has_prev, final):
    q_ref, kl_ref, km_ref, kr_ref, vl_ref, vm_ref, vr_ref = refs[:7]
    refs = refs[7:]
    if has_prev:
        op_ref, lp_ref = refs[:2]
        refs = refs[2:]
    if final:
        o_ref, kwin, vwin, so = refs
    else:
        o_ref, l_ref, kwin, vwin, so, sl = refs
    NSEQ, TL, CW = q_ref.shape
    i = pl.program_id(pos_axis)

    kwin[:, 0:BAND_HALF] = kl_ref[...]
    kwin[:, BAND_HALF:BAND_HALF + TL] = km_ref[...]
    kwin[:, BAND_HALF + TL:] = kr_ref[...]
    vwin[:, 0:BAND_HALF] = vl_ref[...]
    vwin[:, BAND_HALF:BAND_HALF + TL] = vm_ref[...]
    vwin[:, BAND_HALF + TL:] = vr_ref[...]

    lane = lax.broadcasted_iota(jnp.int32, (1, LANES), 1)
    head0 = lane < HEAD_DIM
    m0 = head0.astype(BF16)
    m1 = (1.0 - head0.astype(F32)).astype(BF16)
    row = lax.broadcasted_iota(jnp.int32, (Q_TILE, K_WINDOW), 0)
    col = lax.broadcasted_iota(jnp.int32, (Q_TILE, K_WINDOW), 1)
    band = (col >= row) & (col <= row + 2 * BAND_HALF)
    ones_v = jnp.ones((K_WINDOW, LANES), BF16)

    for j in range(TL // Q_TILE):
        r0 = j * Q_TILE
        kpos = i * TL + (r0 - BAND_HALF) + col
        valid = band & (kpos >= 0) & (kpos < L)
        bias = jnp.where(valid, 0.0, NEG_INF)
        bias2 = jnp.concatenate([bias, bias], axis=0)
        for n in range(NSEQ):
            for hp in range(CW // LANES):
                cs = slice(hp * LANES, (hp + 1) * LANES)
                q = q_ref[n, r0:r0 + Q_TILE, cs]
                k = kwin[n, r0:r0 + K_WINDOW, cs]
                v = vwin[n, r0:r0 + K_WINDOW, cs]
                qq = jnp.concatenate([q * m0, q * m1], axis=0)
                s = lax.dot_general(qq, k, (((1,), (1,)), ((), ())),
                                    preferred_element_type=F32) + bias2
                m = jnp.max(s, axis=1, keepdims=True)
                p = jnp.exp2(s - m).astype(BF16)
                pv = jnp.dot(p, jnp.concatenate([v, ones_v], axis=1),
                             preferred_element_type=F32)
                acc = jnp.where(head0, pv[:Q_TILE, :LANES], pv[Q_TILE:, :LANES])
                ll = jnp.where(head0, pv[:Q_TILE, LANES:], pv[Q_TILE:, LANES:])
                mm = jnp.where(head0, m[:Q_TILE], m[Q_TILE:])
                if has_prev:
                    op = op_ref[n, r0:r0 + Q_TILE, cs].astype(F32)
                    lp = lp_ref[n, r0:r0 + Q_TILE, cs]
                    mn = jnp.maximum(lp, mm)
                    w1 = jnp.exp2(lp - mn)
                    w2 = jnp.exp2(mm - mn)
                    den = w1 + ll * w2
                    o = (op * w1 + acc * w2) / den
                    lse = mn + jnp.log2(den)
                else:
                    o = acc / ll
                    lse = mm + jnp.log2(ll)
                if final:
                    so[hp, pl.ds(n + NSEQ * r0, Q_TILE, stride=NSEQ), :] = o
                else:
                    so[hp, r0:r0 + Q_TILE, :] = o
                    sl[hp, r0:r0 + Q_TILE, :] = lse

    for hp in range(CW // LANES):
        cs = slice(hp * LANES, (hp + 1) * LANES)
        if final:
            o_ref[:, cs] = so[hp].astype(o_ref.dtype)
        else:
            TQ = TL // DIL_STEP
            for r in range(DIL_STEP):
                o_ref[r, :, cs] = so[hp, pl.ds(r, TQ, stride=DIL_STEP), :].astype(o_ref.dtype)
                l_ref[r, :, cs] = sl[hp, pl.ds(r, TQ, stride=DIL_STEP), :]


def _dil_call(qkv, prev, L, grid, blk, in_idx, halo_l, halo_r, state_spec, out_specs, out_shape,
              pos_axis, final, name):
    NSEQ, TL, CW = blk
    main = lambda t: pl.BlockSpec((None, NSEQ, TL, CW), functools.partial(in_idx, t=t))
    left = lambda t: pl.BlockSpec((None, NSEQ, BAND_HALF, CW), functools.partial(halo_l, t=t))
    right = lambda t: pl.BlockSpec((None, NSEQ, BAND_HALF, CW), functools.partial(halo_r, t=t))
    in_specs = [main(0), left(1), main(1), right(1), left(2), main(2), right(2)]
    args = [qkv] * 7
    if prev is not None:
        in_specs += [state_spec, state_spec]
        args += list(prev)
    nhp = CW // LANES
    scratch = [pltpu.VMEM((NSEQ, TL + 2 * BAND_HALF, CW), BF16)] * 2
    scratch += [pltpu.VMEM((nhp, NSEQ * TL, LANES), F32)] * (1 if final else 2)
    return pl.pallas_call(
        functools.partial(_dil_kernel, L=L, pos_axis=pos_axis, has_prev=prev is not None, final=final),
        grid=grid,
        in_specs=in_specs,
        out_specs=out_specs,
        out_shape=out_shape,
        scratch_shapes=scratch,
        compiler_params=pltpu.CompilerParams(
            dimension_semantics=("parallel",) * len(grid), vmem_limit_bytes=VMEM_LIMIT),
        name=name,
    )(*args)


def _dilated_mixture(qkv_nat, qkv_d4, qkv_d16):
    B, S, _ = qkv_nat.shape
    W = DIL_WIDTH
    d4, d16 = DIL_STEP, DIL_STEP * DIL_STEP
    L4, L16 = S // d4, S // d16
    TL = 512
    nb = TL // BAND_HALF

    st4 = lambda dt: jax.ShapeDtypeStruct((B, d4, L4, W), dt)
    spec4 = pl.BlockSpec((None, d4, TL // d4, W), lambda b, i: (b, 0, i, 0))
    state = _dil_call(
        qkv_nat.reshape(B, 1, S, QKV_WIDTH), None, S, (B, S // TL), (1, TL, W),
        lambda b, i, t: (b, 0, i, t),
        lambda b, i, t: (b, 0, jnp.maximum(i * nb - 1, 0), t),
        lambda b, i, t: (b, 0, jnp.minimum((i + 1) * nb, S // BAND_HALF - 1), t),
        None, (spec4, spec4), (st4(BF16), st4(F32)), pos_axis=1, final=False, name="dilated_d1")

    st16 = lambda dt: jax.ShapeDtypeStruct((B, d4, d4, L16, W), dt)
    spec16 = pl.BlockSpec((None, d4, None, TL // d4, W), lambda b, r, i: (b, 0, r, i, 0))
    state = _dil_call(
        qkv_d4, state, L4, (B, d4, L4 // TL), (1, TL, W),
        lambda b, r, i, t: (b, r, i, t),
        lambda b, r, i, t: (b, r, jnp.maximum(i * nb - 1, 0), t),
        lambda b, r, i, t: (b, r, jnp.minimum((i + 1) * nb, L4 // BAND_HALF - 1), t),
        pl.BlockSpec((None, 1, TL, W), lambda b, r, i: (b, r, i, 0)),
        (spec16, spec16), (st16(BF16), st16(F32)), pos_axis=2, final=False, name="dilated_d4")
    state = tuple(s.reshape(B, d16, L16, W) for s in state)

    CW = 2 * LANES
    nbq = Q_TILE // BAND_HALF
    span = d16 * Q_TILE
    return _dil_call(
        qkv_d16, state, L16, (B, S // span, W // CW), (d16, Q_TILE, CW),
        lambda b, i, c, t: (b, 0, i, t * (W // CW) + c),
        lambda b, i, c, t: (b, 0, jnp.maximum(i * nbq - 1, 0), t * (W // CW) + c),
        lambda b, i, c, t: (b, 0, jnp.minimum((i + 1) * nbq, L16 // BAND_HALF - 1), t * (W // CW) + c),
        pl.BlockSpec((None, d16, Q_TILE, CW), lambda b, i, c: (b, 0, i, c)),
        pl.BlockSpec((None, span, CW), lambda b, i, c: (b, i, c)),
        jax.ShapeDtypeStruct((B, S, W), BF16), pos_axis=1, final=True, name="dilated_d16")


def _gla_kernel(*refs, TS, forward):
    if forward:
        q_ref, k_ref, la_ref, v_ref, ob_ref, r_ref, g_ref, o_ref, state_ref = refs
    else:
        q_ref, k_ref, la_ref, v_ref, o_ref, state_ref = refs
    C = GLA_CHUNK
    nch = TS // C

    @pl.when(pl.program_id(2) == 0)
    def _():
        state_ref[...] = jnp.zeros_like(state_ref)

    lane = lax.broadcasted_iota(jnp.int32, (1, LANES), 1)
    m0 = (lane < GLA_DK).astype(F32)
    m1 = 1.0 - m0
    tr = lax.broadcasted_iota(jnp.int32, (C, C), 0)
    tc = lax.broadcasted_iota(jnp.int32, (C, C), 1)
    keep = (tc <= tr) if forward else (tc > tr)
    keep2 = jnp.concatenate([keep, keep], axis=0)[None]

    b = la_ref[0]
    rowc = lax.broadcasted_iota(jnp.int32, (TS, LANES), 0) % C
    step = 1
    while step < C:
        if forward:
            b = b + jnp.where(rowc >= step, pltpu.roll(b, step, 0), 0.0)
        else:
            b = b + jnp.where(rowc < C - step, pltpu.roll(b, TS - step, 0), 0.0)
        step *= 2
    b = b.reshape(nch, C, LANES)
    total = b[:, C - 1:C, :] if forward else b[:, 0:1, :]

    q = q_ref[0].reshape(nch, C, LANES)
    k = k_ref[0].reshape(nch, C, LANES)
    v = v_ref[0].reshape(nch, C, 2 * GLA_DV)
    q_in = q * jnp.exp(b)
    k_in = (k * jnp.exp(-b)).astype(BF16)
    k_dec = k * jnp.exp(total - b)
    qs = jnp.concatenate([q_in * m0, q_in * m1], axis=1).astype(BF16)
    att = jnp.einsum("cqd,ckd->cqk", qs, k_in, preferred_element_type=F32)
    att = jnp.where(keep2, att, 0.0).astype(BF16)
    intra = jnp.einsum("cqk,ckv->cqv", att, v, preferred_element_type=F32)
    ks = jnp.concatenate([k_dec * m0, k_dec * m1], axis=1).astype(BF16)
    vs = jnp.concatenate([v[:, :, :GLA_DV], v[:, :, GLA_DV:]], axis=1)
    kv_t = jnp.einsum("ctv,ctl->cvl", vs, ks, preferred_element_type=F32)

    decay = jnp.exp(total)
    state_t = state_ref[...]
    states = [None] * nch
    for c in (range(nch) if forward else range(nch - 1, -1, -1)):
        states[c] = state_t.astype(BF16)
        state_t = state_t * decay[c] + kv_t[c]
    state_ref[...] = state_t
    inter = jnp.einsum("cql,cvl->cqv", qs, jnp.stack(states), preferred_element_type=F32)
    o = jnp.concatenate([intra[:, :C, :GLA_DV] + inter[:, :C], intra[:, C:, GLA_DV:] + inter[:, C:]],
                        axis=2).reshape(TS, 2 * GLA_DV)
    if forward:
        tot = o + ob_ref[0].astype(F32)
        r = r_ref[0].astype(F32)
        gate = r / (1.0 + jnp.exp(-r))
        for hh in range(2):
            hs = slice(hh * GLA_DV, (hh + 1) * GLA_DV)
            o_ref[0, :, hs] = (_rms(tot[:, hs], g_ref[...]) * gate[:, hs]).astype(o_ref.dtype)
    else:
        o_ref[0] = o.astype(o_ref.dtype)


def _gla_pass(gq, gk, la, gv, ts, forward, ob=None, gr=None, g_gla=None):
    B, S, _ = gq.shape
    nt = S // ts
    tile = (lambda i: i) if forward else (lambda i: nt - 1 - i)
    la_off = 0 if forward else GLA_KEY_WIDTH // LANES
    key = pl.BlockSpec((1, ts, LANES), lambda b, h, i: (b, tile(i), h))
    gate = pl.BlockSpec((1, ts, LANES), lambda b, h, i: (b, tile(i), la_off + h))
    val = pl.BlockSpec((1, ts, 2 * GLA_DV), lambda b, h, i: (b, tile(i), h))
    in_specs = [key, key, gate, val]
    args = [gq, gk, la, gv]
    if forward:
        in_specs += [val, val, pl.BlockSpec(g_gla.shape, lambda b, h, i: (0, 0))]
        args += [ob, gr, g_gla]
    return pl.pallas_call(
        functools.partial(_gla_kernel, TS=ts, forward=forward),
        grid=(B, N_HEADS_GLA // 2, nt),
        in_specs=in_specs,
        out_specs=val,
        out_shape=jax.ShapeDtypeStruct((B, S, GLA_WIDTH), BF16),
        scratch_shapes=[pltpu.VMEM((GLA_DV, 2 * GLA_DK), F32)],
        compiler_params=pltpu.CompilerParams(
            dimension_semantics=("parallel", "parallel", "arbitrary"), vmem_limit_bytes=VMEM_LIMIT),
        name="gla_fwd" if forward else "gla_bwd",
    )(*args)


def _tail_kernel(x_ref, od_ref, og_ref, wo_ref, gf_ref, wg_ref, wu_ref, wd_ref, gl_ref, y_ref, *, ff_chunk):
    mix = (jnp.dot(od_ref[...], wo_ref[:DIL_WIDTH, :], preferred_element_type=F32)
           + jnp.dot(og_ref[...], wo_ref[DIL_WIDTH:, :], preferred_element_type=F32))
    x1 = x_ref[...] + mix
    h = _rms(x1, gf_ref[...]).astype(BF16)
    acc = x1
    for c0 in range(0, D_FF, ff_chunk):
        gate = jnp.dot(h, wg_ref[:, c0:c0 + ff_chunk], preferred_element_type=F32)
        up = jnp.dot(h, wu_ref[:, c0:c0 + ff_chunk], preferred_element_type=F32)
        act = (gate / (1.0 + jnp.exp(-gate)) * up).astype(BF16)
        acc = acc + jnp.dot(act, wd_ref[c0:c0 + ff_chunk, :], preferred_element_type=F32)
    y_ref[...] = _rms(acc, gl_ref[...])


def _tail(x, o_dil, o_gla, w_out, g_ffn, w_gate, w_up, w_down, g_final, tm, ff_chunk):
    N, D = x.shape
    tok = lambda w: pl.BlockSpec((tm, w), lambda i: (i, 0))
    full = lambda a: pl.BlockSpec(a.shape, lambda i: (0, 0), pipeline_mode=pl.Buffered(1))
    return pl.pallas_call(
        functools.partial(_tail_kernel, ff_chunk=ff_chunk),
        grid=(N // tm,),
        in_specs=[tok(D), tok(DIL_WIDTH), tok(GLA_WIDTH), full(w_out), full(g_ffn),
                  full(w_gate), full(w_up), full(w_down), full(g_final)],
        out_specs=tok(D),
        out_shape=jax.ShapeDtypeStruct((N, D), F32),
        compiler_params=pltpu.CompilerParams(
            dimension_semantics=("parallel",), vmem_limit_bytes=VMEM_LIMIT),
        name="tail",
    )(x, o_dil, o_gla, w_out, g_ffn, w_gate, w_up, w_down, g_final)


def _trunk(x, p, tm_in=512, ts_gla=1024, tm_tail=512, ff_chunk=D_FF // 2):
    B, S, D = x.shape
    qkv_nat, qkv_d4, qkv_d16, gq, gk, gv, gr, la = _inproj(
        x, p["g_mix"], p["w_in"], p["w_gate"], p["b_gate"], tm_in)
    o_dil = _dilated_mixture(qkv_nat, qkv_d4, qkv_d16)
    ob = _gla_pass(gq, gk, la, gv, ts_gla, forward=False)
    o_gla = _gla_pass(gq, gk, la, gv, ts_gla, forward=True, ob=ob, gr=gr, g_gla=p["g_gla"])
    y = _tail(x.reshape(B * S, D), o_dil.reshape(B * S, DIL_WIDTH), o_gla.reshape(B * S, GLA_WIDTH),
              p["w_out"], p["g_ffn"], p["w_ffn_gate"], p["w_ffn_up"], p["w_ffn_down"], p["g_final"],
              tm_tail, ff_chunk)
    return y.reshape(B, S, D)


def _prepare_params(g_mix, w_in, w_gate_f, b_gate_f, w_gate_b, b_gate_b, g_gla, w_out,
                    g_ffn, w_ffn_gate, w_ffn_up, w_ffn_down, g_final):
    z = jnp.zeros_like(w_gate_f[0])
    w_gate = jnp.concatenate(
        [jnp.concatenate([w_gate_f[0], z], axis=1), jnp.concatenate([z, w_gate_b[0]], axis=1)], axis=0)
    return {
        "g_mix": g_mix[0][None, :],
        "w_in": w_in[0].astype(BF16),
        "w_gate": w_gate.astype(BF16),
        "b_gate": jnp.concatenate([b_gate_f[0], b_gate_b[0]])[None, :],
        "g_gla": g_gla[0][None, :],
        "w_out": w_out[0].astype(BF16),
        "g_ffn": g_ffn[0][None, :],
        "w_ffn_gate": w_ffn_gate[0].astype(BF16),
        "w_ffn_up": w_ffn_up[0].astype(BF16),
        "w_ffn_down": w_ffn_down[0].astype(BF16),
        "g_final": g_final[None, :],
    }


def kernel(x_prompt, x_sample, g_mix, w_in, w_gate_f, b_gate_f, w_gate_b, b_gate_b, g_gla, w_out,
           g_ffn, w_ffn_gate, w_ffn_up, w_ffn_down, g_final):
    p = _prepare_params(g_mix, w_in, w_gate_f, b_gate_f, w_gate_b, b_gate_b, g_gla, w_out,
                        g_ffn, w_ffn_gate, w_ffn_up, w_ffn_down, g_final)
    return (_trunk(x_prompt, p), _trunk(x_sample, p))
```

```python
import functools

import jax
import jax.numpy as jnp
from jax import lax
from jax.experimental import pallas as pl
from jax.experimental.pallas import tpu as pltpu

D_MODEL = 1024
HEAD_DIM = 64
N_HEADS_DIL = 8
DIL_WIDTH = N_HEADS_DIL * HEAD_DIM
N_HEADS_GLA = 4
GLA_DK = 64
GLA_DV = 128
GLA_KEY_WIDTH = N_HEADS_GLA * GLA_DK
GLA_WIDTH = N_HEADS_GLA * GLA_DV
GLA_RANK = 16
GLA_TAU = 16.0
GLA_CHUNK = 64
DILATED_PATTERNS = ((128, 1), (512, 4), (2048, 16))
ROT_DIM = HEAD_DIM // 4
ROPE_THETA = 500000.0
D_FF = 2816
EPS = 1e-6
NEG_INF = -1e30
LOG2_E = 1.4426950408889634
PROJ_WIDTH = 3 * DIL_WIDTH + 2 * GLA_KEY_WIDTH + 2 * GLA_WIDTH + 2 * GLA_RANK
QKV_WIDTH = 3 * DIL_WIDTH

LANES = 128
BAND_HALF = 64
Q_TILE = 128
K_WINDOW = Q_TILE + 2 * BAND_HALF
DIL_STEP = 4
VMEM_LIMIT = 56 * 1024 * 1024

F32 = jnp.float32
BF16 = jnp.bfloat16


def _rms(x, g):
    return x * lax.rsqrt(jnp.mean(x * x, axis=-1, keepdims=True) + EPS) * g


def _inproj_kernel(x_ref, g_ref, w_ref, wg_ref, bg_ref, cos_ref, sa_ref, sb_ref,
                   nat_ref, d4_ref, d16_ref, gq_ref, gk_ref, gv_ref, gr_ref, la_ref,
                   slab, slab4):
    TM = x_ref.shape[1]
    R4 = TM // DIL_STEP
    R16 = R4 // DIL_STEP
    h = _rms(x_ref[0], g_ref[...]).astype(BF16)

    def proj(lo, hi):
        return jnp.dot(h, w_ref[:, lo:hi], preferred_element_type=F32)

    cos_t, sin_a, sin_b = cos_ref[...], sa_ref[...], sb_ref[...]

    def emit(yb, blk):
        cs = slice(blk * LANES, (blk + 1) * LANES)
        nat_ref[0, :, cs] = yb.astype(BF16)
        slab[blk] = yb
        for r in range(DIL_STEP):
            slab4[blk, r * R4:(r + 1) * R4, :] = slab[blk, pl.ds(r, R4, stride=DIL_STEP), :]
        for r in range(DIL_STEP):
            d4_ref[0, r, :, cs] = slab4[blk, r * R4:(r + 1) * R4, :].astype(BF16)
            for j in range(DIL_STEP):
                d16_ref[0, r + DIL_STEP * j, :, cs] = slab4[
                    blk, pl.ds(r * R4 + j, R16, stride=DIL_STEP), :].astype(BF16)

    def rotary(xb):
        return (xb * cos_t + pltpu.roll(xb, LANES - ROT_DIM // 2, 1) * sin_a
                + pltpu.roll(xb, ROT_DIM // 2, 1) * sin_b)

    o = PROJ_WIDTH - 2 * GLA_RANK
    g_lr = proj(o, o + 2 * GLA_RANK).astype(BF16)
    pre = jnp.dot(g_lr, wg_ref[...], preferred_element_type=F32) + bg_ref[...]
    la_ref[0] = (jnp.minimum(pre, 0.0) - jnp.log(1.0 + jnp.exp(-jnp.abs(pre)))) * (1.0 / GLA_TAU)

    nblk = DIL_WIDTH // LANES
    t = proj(0, DIL_WIDTH)
    for c in range(nblk):
        emit(rotary(t[:, c * LANES:(c + 1) * LANES]) * (HEAD_DIM ** -0.5 * LOG2_E), c)
    t = proj(DIL_WIDTH, 2 * DIL_WIDTH)
    for c in range(nblk):
        emit(rotary(t[:, c * LANES:(c + 1) * LANES]), nblk + c)
    t = proj(2 * DIL_WIDTH, 3 * DIL_WIDTH)
    for c in range(nblk):
        emit(t[:, c * LANES:(c + 1) * LANES], 2 * nblk + c)
    o = QKV_WIDTH
    gq_ref[0] = (proj(o, o + GLA_KEY_WIDTH) * (GLA_DK ** -0.5)).astype(BF16)
    o += GLA_KEY_WIDTH
    gk_ref[0] = proj(o, o + GLA_KEY_WIDTH).astype(BF16)
    o += GLA_KEY_WIDTH
    gv_ref[0] = proj(o, o + GLA_WIDTH).astype(BF16)
    o += GLA_WIDTH
    gr_ref[0] = proj(o, o + GLA_WIDTH).astype(BF16)


def _rotary_tables(S):
    half = ROT_DIM // 2
    inv_freq = jnp.power(jnp.float32(ROPE_THETA), -jnp.arange(0, ROT_DIM, 2, dtype=F32) / ROT_DIM)
    ang = jnp.arange(S, dtype=F32)[:, None] * inv_freq[None, :]
    cos, sin = jnp.cos(ang), jnp.sin(ang)
    ones = jnp.ones((S, HEAD_DIM - ROT_DIM), F32)
    zeros = jnp.zeros((S, HEAD_DIM - ROT_DIM), F32)
    zh = jnp.zeros((S, half), F32)
    cos_t = jnp.concatenate([cos, cos, ones], axis=1)
    sin_a = jnp.concatenate([-sin, zh, zeros], axis=1)
    sin_b = jnp.concatenate([zh, sin, zeros], axis=1)
    rep = LANES // HEAD_DIM
    return tuple(jnp.tile(t, (1, rep)) for t in (cos_t, sin_a, sin_b))


def _inproj(x, g_mix, w_in, w_gate, b_gate, tm):
    B, S, D = x.shape
    d4, d16 = DIL_STEP, DIL_STEP * DIL_STEP
    cos_t, sin_a, sin_b = _rotary_tables(S)
    tok = lambda w: pl.BlockSpec((1, tm, w), lambda b, i: (b, i, 0))
    res = lambda d: pl.BlockSpec((1, d, tm // d, QKV_WIDTH), lambda b, i: (b, 0, i, 0))
    full = lambda a: pl.BlockSpec(a.shape, lambda b, i: (0,) * a.ndim)
    tab = pl.BlockSpec((tm, LANES), lambda b, i: (i, 0))
    out_shape = (
        jax.ShapeDtypeStruct((B, S, QKV_WIDTH), BF16),
        jax.ShapeDtypeStruct((B, d4, S // d4, QKV_WIDTH), BF16),
        jax.ShapeDtypeStruct((B, d16, S // d16, QKV_WIDTH), BF16),
        jax.ShapeDtypeStruct((B, S, GLA_KEY_WIDTH), BF16),
        jax.ShapeDtypeStruct((B, S, GLA_KEY_WIDTH), BF16),
        jax.ShapeDtypeStruct((B, S, GLA_WIDTH), BF16),
        jax.ShapeDtypeStruct((B, S, GLA_WIDTH), BF16),
        jax.ShapeDtypeStruct((B, S, 2 * GLA_KEY_WIDTH), F32),
    )
    out_specs = (tok(QKV_WIDTH), res(d4), res(d16)) + tuple(tok(s.shape[-1]) for s in out_shape[3:])
    nslab = QKV_WIDTH // LANES
    return pl.pallas_call(
        _inproj_kernel,
        grid=(B, S // tm),
        in_specs=[tok(D), full(g_mix), full(w_in), full(w_gate), full(b_gate), tab, tab, tab],
        out_specs=out_specs,
        out_shape=out_shape,
        scratch_shapes=[pltpu.VMEM((nslab, tm, LANES), F32)] * 2,
        compiler_params=pltpu.CompilerParams(
            dimension_semantics=("parallel", "parallel"), vmem_limit_bytes=VMEM_LIMIT),
        name="inproj",
    )(x, g_mix, w_in, w_gate, b_gate, cos_t, sin_a, sin_b)


def _dil_kernel(*refs, L, pos_axis, has_prev, final):
    q_ref, kl_ref, km_ref, kr_ref, vl_ref, vm_ref, vr_ref = refs[:7]
    refs = refs[7:]
    if has_prev:
        op_ref, lp_ref = refs[:2]
        refs = refs[2:]
    if final:
        o_ref, kwin, vwin, so = refs
    else:
        o_ref, l_ref, kwin, vwin, so, sl = refs
    NSEQ, TL, CW = q_ref.shape
    i = pl.program_id(pos_axis)

    kwin[:, 0:BAND_HALF] = kl_ref[...]
    kwin[:, BAND_HALF:BAND_HALF + TL] = km_ref[...]
    kwin[:, BAND_HALF + TL:] = kr_ref[...]
    vwin[:, 0:BAND_HALF] = vl_ref[...]
    vwin[:, BAND_HALF:BAND_HALF + TL] = vm_ref[...]
    vwin[:, BAND_HALF + TL:] = vr_ref[...]

    lane = lax.broadcasted_iota(jnp.int32, (1, LANES), 1)
    head0 = lane < HEAD_DIM
    m0 = head0.astype(BF16)
    m1 = (1.0 - head0.astype(F32)).astype(BF16)
    row = lax.broadcasted_iota(jnp.int32, (Q_TILE, K_WINDOW), 0)
    col = lax.broadcasted_iota(jnp.int32, (Q_TILE, K_WINDOW), 1)
    band = (col >= row) & (col <= row + 2 * BAND_HALF)
    ones_v = jnp.ones((K_WINDOW, LANES), BF16)

    for j in range(TL // Q_TILE):
        r0 = j * Q_TILE
        kpos = i * TL + (r0 - BAND_HALF) + col
        valid = band & (kpos >= 0) & (kpos < L)
        bias = jnp.where(valid, 0.0, NEG_INF)
        bias2 = jnp.concatenate([bias, bias], axis=0)
        for n in range(NSEQ):
            for hp in range(CW // LANES):
                cs = slice(hp * LANES, (hp + 1) * LANES)
                q = q_ref[n, r0:r0 + Q_TILE, cs]
                k = kwin[n, r0:r0 + K_WINDOW, cs]
                v = vwin[n, r0:r0 + K_WINDOW, cs]
                qq = jnp.concatenate([q * m0, q * m1], axis=0)
                s = lax.dot_general(qq, k, (((1,), (1,)), ((), ())),
                                    preferred_element_type=F32) + bias2
                m = jnp.max(s, axis=1, keepdims=True)
                p = jnp.exp2(s - m).astype(BF16)
                pv = jnp.dot(p, jnp.concatenate([v, ones_v], axis=1),
                             preferred_element_type=F32)
                acc = jnp.where(head0, pv[:Q_TILE, :LANES], pv[Q_TILE:, :LANES])
                ll = jnp.where(head0, pv[:Q_TILE, LANES:], pv[Q_TILE:, LANES:])
                mm = jnp.where(head0, m[:Q_TILE], m[Q_TILE:])
                if has_prev:
                    op = op_ref[n, r0:r0 + Q_TILE, cs].astype(F32)
                    lp = lp_ref[n, r0:r0 + Q_TILE, cs]
                    mn = jnp.maximum(lp, mm)
                    w1 = jnp.exp2(lp - mn)
                    w2 = jnp.exp2(mm - mn)
                    den = w1 + ll * w2
                    o = (op * w1 + acc * w2) / den
                    lse = mn + jnp.log2(den)
                else:
                    o = acc / ll
                    lse = mm + jnp.log2(ll)
                if final:
                    so[hp, pl.ds(n + NSEQ * r0, Q_TILE, stride=NSEQ), :] = o
                else:
                    so[hp, r0:r0 + Q_TILE, :] = o
                    sl[hp, r0:r0 + Q_TILE, :] = lse

    for hp in range(CW // LANES):
        cs = slice(hp * LANES, (hp + 1) * LANES)
        if final:
            o_ref[:, cs] = so[hp].astype(o_ref.dtype)
        else:
            TQ = TL // DIL_STEP
            for r in range(DIL_STEP):
                o_ref[r, :, cs] = so[hp, pl.ds(r, TQ, stride=DIL_STEP), :].astype(o_ref.dtype)
                l_ref[r, :, cs] = sl[hp, pl.ds(r, TQ, stride=DIL_STEP), :]


def _dil_call(qkv, prev, L, grid, blk, in_idx, halo_l, halo_r, state_spec, out_specs, out_shape,
              pos_axis, final, name):
    NSEQ, TL, CW = blk
    main = lambda t: pl.BlockSpec((None, NSEQ, TL, CW), functools.partial(in_idx, t=t))
    left = lambda t: pl.BlockSpec((None, NSEQ, BAND_HALF, CW), functools.partial(halo_l, t=t))
    right = lambda t: pl.BlockSpec((None, NSEQ, BAND_HALF, CW), functools.partial(halo_r, t=t))
    in_specs = [main(0), left(1), main(1), right(1), left(2), main(2), right(2)]
    args = [qkv] * 7
    if prev is not None:
        in_specs += [state_spec, state_spec]
        args += list(prev)
    nhp = CW // LANES
    scratch = [pltpu.VMEM((NSEQ, TL + 2 * BAND_HALF, CW), BF16)] * 2
    scratch += [pltpu.VMEM((nhp, NSEQ * TL, LANES), F32)] * (1 if final else 2)
    return pl.pallas_call(
        functools.partial(_dil_kernel, L=L, pos_axis=pos_axis, has_prev=prev is not None, final=final),
        grid=grid,
        in_specs=in_specs,
        out_specs=out_specs,
        out_shape=out_shape,
        scratch_shapes=scratch,
        compiler_params=pltpu.CompilerParams(
            dimension_semantics=("parallel",) * len(grid), vmem_limit_bytes=VMEM_LIMIT),
        name=name,
    )(*args)


def _dilated_mixture(qkv_nat, qkv_d4, qkv_d16):
    B, S, _ = qkv_nat.shape
    W = DIL_WIDTH
    d4, d16 = DIL_STEP, DIL_STEP * DIL_STEP
    L4, L16 = S // d4, S // d16
    TL = 1024
    nb = TL // BAND_HALF

    st4 = lambda dt: jax.ShapeDtypeStruct((B, d4, L4, W), dt)
    spec4 = pl.BlockSpec((None, d4, TL // d4, W), lambda b, i: (b, 0, i, 0))
    state = _dil_call(
        qkv_nat.reshape(B, 1, S, QKV_WIDTH), None, S, (B, S // TL), (1, TL, W),
        lambda b, i, t: (b, 0, i, t),
        lambda b, i, t: (b, 0, jnp.maximum(i * nb - 1, 0), t),
        lambda b, i, t: (b, 0, jnp.minimum((i + 1) * nb, S // BAND_HALF - 1), t),
        None, (spec4, spec4), (st4(BF16), st4(F32)), pos_axis=1, final=False, name="dilated_d1")

    st16 = lambda dt: jax.ShapeDtypeStruct((B, d4, d4, L16, W), dt)
    spec16 = pl.BlockSpec((None, d4, None, TL // d4, W), lambda b, r, i: (b, 0, r, i, 0))
    state = _dil_call(
        qkv_d4, state, L4, (B, d4, L4 // TL), (1, TL, W),
        lambda b, r, i, t: (b, r, i, t),
        lambda b, r, i, t: (b, r, jnp.maximum(i * nb - 1, 0), t),
        lambda b, r, i, t: (b, r, jnp.minimum((i + 1) * nb, L4 // BAND_HALF - 1), t),
        pl.BlockSpec((None, 1, TL, W), lambda b, r, i: (b, r, i, 0)),
        (spec16, spec16), (st16(BF16), st16(F32)), pos_axis=2, final=False, name="dilated_d4")
    state = tuple(s.reshape(B, d16, L16, W) for s in state)

    CW = 2 * LANES
    nbq = Q_TILE // BAND_HALF
    span = d16 * Q_TILE
    return _dil_call(
        qkv_d16, state, L16, (B, S // span, W // CW), (d16, Q_TILE, CW),
        lambda b, i, c, t: (b, 0, i, t * (W // CW) + c),
        lambda b, i, c, t: (b, 0, jnp.maximum(i * nbq - 1, 0), t * (W // CW) + c),
        lambda b, i, c, t: (b, 0, jnp.minimum((i + 1) * nbq, L16 // BAND_HALF - 1), t * (W // CW) + c),
        pl.BlockSpec((None, d16, Q_TILE, CW), lambda b, i, c: (b, 0, i, c)),
        pl.BlockSpec((None, span, CW), lambda b, i, c: (b, i, c)),
        jax.ShapeDtypeStruct((B, S, W), BF16), pos_axis=1, final=True, name="dilated_d16")


def _gla_kernel(*refs, TS, forward):
    if forward:
        q_ref, k_ref, la_ref, v_ref, ob_ref, r_ref, g_ref, o_ref, state_ref = refs
    else:
        q_ref, k_ref, la_ref, v_ref, o_ref, state_ref = refs
    C = GLA_CHUNK
    nch = TS // C

    @pl.when(pl.program_id(2) == 0)
    def _():
        state_ref[...] = jnp.zeros_like(state_ref)

    lane = lax.broadcasted_iota(jnp.int32, (1, LANES), 1)
    m0 = (lane < GLA_DK).astype(F32)
    m1 = 1.0 - m0
    tr = lax.broadcasted_iota(jnp.int32, (C, C), 0)
    tc = lax.broadcasted_iota(jnp.int32, (C, C), 1)
    keep = (tc <= tr) if forward else (tc > tr)
    keep2 = jnp.concatenate([keep, keep], axis=0)[None]

    b = la_ref[0]
    rowc = lax.broadcasted_iota(jnp.int32, (TS, LANES), 0) % C
    step = 1
    while step < C:
        if forward:
            b = b + jnp.where(rowc >= step, pltpu.roll(b, step, 0), 0.0)
        else:
            b = b + jnp.where(rowc < C - step, pltpu.roll(b, TS - step, 0), 0.0)
        step *= 2
    b = b.reshape(nch, C, LANES)
    total = b[:, C - 1:C, :] if forward else b[:, 0:1, :]

    q = q_ref[0].reshape(nch, C, LANES)
    k = k_ref[0].reshape(nch, C, LANES)
    v = v_ref[0].reshape(nch, C, 2 * GLA_DV)
    q_in = q * jnp.exp(b)
    k_in = (k * jnp.exp(-b)).astype(BF16)
    k_dec = k * jnp.exp(total - b)
    qs = jnp.concatenate([q_in * m0, q_in * m1], axis=1).astype(BF16)
    att = jnp.einsum("cqd,ckd->cqk", qs, k_in, preferred_element_type=F32)
    att = jnp.where(keep2, att, 0.0).astype(BF16)
    intra = jnp.einsum("cqk,ckv->cqv", att, v, preferred_element_type=F32)
    ks = jnp.concatenate([k_dec * m0, k_dec * m1], axis=1).astype(BF16)
    vs = jnp.concatenate([v[:, :, :GLA_DV], v[:, :, GLA_DV:]], axis=1)
    kv_t = jnp.einsum("ctv,ctl->cvl", vs, ks, preferred_element_type=F32)

    decay = jnp.exp(total)
    state_t = state_ref[...]
    states = [None] * nch
    for c in (range(nch) if forward else range(nch - 1, -1, -1)):
        states[c] = state_t.astype(BF16)
        state_t = state_t * decay[c] + kv_t[c]
    state_ref[...] = state_t
    inter = jnp.einsum("cql,cvl->cqv", qs, jnp.stack(states), preferred_element_type=F32)
    o = jnp.concatenate([intra[:, :C, :GLA_DV] + inter[:, :C], intra[:, C:, GLA_DV:] + inter[:, C:]],
                        axis=2).reshape(TS, 2 * GLA_DV)
    if forward:
        tot = o + ob_ref[0].astype(F32)
        r = r_ref[0].astype(F32)
        gate = r / (1.0 + jnp.exp(-r))
        for hh in range(2):
            hs = slice(hh * GLA_DV, (hh + 1) * GLA_DV)
            o_ref[0, :, hs] = (_rms(tot[:, hs], g_ref[...]) * gate[:, hs]).astype(o_ref.dtype)
    else:
        o_ref[0] = o.astype(o_ref.dtype)


def _gla_pass(gq, gk, la, gv, ts, forward, ob=None, gr=None, g_gla=None):
    B, S, _ = gq.shape
    nt = S // ts
    tile = (lambda i: i) if forward else (lambda i: nt - 1 - i)
    la_off = 0 if forward else GLA_KEY_WIDTH // LANES
    key = pl.BlockSpec((1, ts, LANES), lambda b, h, i: (b, tile(i), h))
    gate = pl.BlockSpec((1, ts, LANES), lambda b, h, i: (b, tile(i), la_off + h))
    val = pl.BlockSpec((1, ts, 2 * GLA_DV), lambda b, h, i: (b, tile(i), h))
    in_specs = [key, key, gate, val]
    args = [gq, gk, la, gv]
    if forward:
        in_specs += [val, val, pl.BlockSpec(g_gla.shape, lambda b, h, i: (0, 0))]
        args += [ob, gr, g_gla]
    return pl.pallas_call(
        functools.partial(_gla_kernel, TS=ts, forward=forward),
        grid=(B, N_HEADS_GLA // 2, nt),
        in_specs=in_specs,
        out_specs=val,
        out_shape=jax.ShapeDtypeStruct((B, S, GLA_WIDTH), BF16),
        scratch_shapes=[pltpu.VMEM((GLA_DV, 2 * GLA_DK), F32)],
        compiler_params=pltpu.CompilerParams(
            dimension_semantics=("parallel", "parallel", "arbitrary"), vmem_limit_bytes=VMEM_LIMIT),
        name="gla_fwd" if forward else "gla_bwd",
    )(*args)


def _tail_kernel(x_ref, od_ref, og_ref, wo_ref, gf_ref, wg_ref, wu_ref, wd_ref, gl_ref, y_ref, *, ff_chunk):
    mix = (jnp.dot(od_ref[...], wo_ref[:DIL_WIDTH, :], preferred_element_type=F32)
           + jnp.dot(og_ref[...], wo_ref[DIL_WIDTH:, :], preferred_element_type=F32))
    x1 = x_ref[...] + mix
    h = _rms(x1, gf_ref[...]).astype(BF16)
    acc = x1
    for c0 in range(0, D_FF, ff_chunk):
        gate = jnp.dot(h, wg_ref[:, c0:c0 + ff_chunk], preferred_element_type=F32)
        up = jnp.dot(h, wu_ref[:, c0:c0 + ff_chunk], preferred_element_type=F32)
        act = (gate / (1.0 + jnp.exp(-gate)) * up).astype(BF16)
        acc = acc + jnp.dot(act, wd_ref[c0:c0 + ff_chunk, :], preferred_element_type=F32)
    y_ref[...] = _rms(acc, gl_ref[...])


def _tail(x, o_dil, o_gla, w_out, g_ffn, w_gate, w_up, w_down, g_final, tm, ff_chunk):
    N, D = x.shape
    tok = lambda w: pl.BlockSpec((tm, w), lambda i: (i, 0))
    full = lambda a: pl.BlockSpec(a.shape, lambda i: (0, 0), pipeline_mode=pl.Buffered(1))
    return pl.pallas_call(
        functools.partial(_tail_kernel, ff_chunk=ff_chunk),
        grid=(N // tm,),
        in_specs=[tok(D), tok(DIL_WIDTH), tok(GLA_WIDTH), full(w_out), full(g_ffn),
                  full(w_gate), full(w_up), full(w_down), full(g_final)],
        out_specs=tok(D),
        out_shape=jax.ShapeDtypeStruct((N, D), F32),
        compiler_params=pltpu.CompilerParams(
            dimension_semantics=("parallel",), vmem_limit_bytes=VMEM_LIMIT),
        name="tail",
    )(x, o_dil, o_gla, w_out, g_ffn, w_gate, w_up, w_down, g_final)


def _trunk(x, p, tm_in=512, ts_gla=2048, tm_tail=1024, ff_chunk=256):
    B, S, D = x.shape
    qkv_nat, qkv_d4, qkv_d16, gq, gk, gv, gr, la = _inproj(
        x, p["g_mix"], p["w_in"], p["w_gate"], p["b_gate"], tm_in)
    o_dil = _dilated_mixture(qkv_nat, qkv_d4, qkv_d16)
    ob = _gla_pass(gq, gk, la, gv, ts_gla, forward=False)
    o_gla = _gla_pass(gq, gk, la, gv, ts_gla, forward=True, ob=ob, gr=gr, g_gla=p["g_gla"])
    y = _tail(x.reshape(B * S, D), o_dil.reshape(B * S, DIL_WIDTH), o_gla.reshape(B * S, GLA_WIDTH),
              p["w_out"], p["g_ffn"], p["w_ffn_gate"], p["w_ffn_up"], p["w_ffn_down"], p["g_final"],
              tm_tail, ff_chunk)
    return y.reshape(B, S, D)


def _prepare_params(g_mix, w_in, w_gate_f, b_gate_f, w_gate_b, b_gate_b, g_gla, w_out,
                    g_ffn, w_ffn_gate, w_ffn_up, w_ffn_down, g_final):
    z = jnp.zeros_like(w_gate_f[0])
    w_gate = jnp.concatenate(
        [jnp.concatenate([w_gate_f[0], z], axis=1), jnp.concatenate([z, w_gate_b[0]], axis=1)], axis=0)
    return {
        "g_mix": g_mix[0][None, :],
        "w_in": w_in[0].astype(BF16),
        "w_gate": w_gate.astype(BF16),
        "b_gate": jnp.concatenate([b_gate_f[0], b_gate_b[0]])[None, :],
        "g_gla": g_gla[0][None, :],
        "w_out": w_out[0].astype(BF16),
        "g_ffn": g_ffn[0][None, :],
        "w_ffn_gate": w_ffn_gate[0].astype(BF16),
        "w_ffn_up": w_ffn_up[0].astype(BF16),
        "w_ffn_down": w_ffn_down[0].astype(BF16),
        "g_final": g_final[None, :],
    }


def kernel(x_prompt, x_sample, g_mix, w_in, w_gate_f, b_gate_f, w_gate_b, b_gate_b, g_gla, w_out,
           g_ffn, w_ffn_gate, w_ffn_up, w_ffn_down, g_final):
    p = _prepare_params(g_mix, w_in, w_gate_f, b_gate_f, w_gate_b, b_gate_b, g_gla, w_out,
                        g_ffn, w_ffn_gate, w_ffn_up, w_ffn_down, g_final)
    return (_trunk(x_prompt, p), _trunk(x_sample, p))
```

```python
import functools

import jax
import jax.numpy as jnp
from jax import lax
from jax.experimental import pallas as pl
from jax.experimental.pallas import tpu as pltpu

D_MODEL = 1024
HEAD_DIM = 64
N_HEADS_DIL = 8
DIL_WIDTH = N_HEADS_DIL * HEAD_DIM
N_HEADS_GLA = 4
GLA_DK = 64
GLA_DV = 128
GLA_KEY_WIDTH = N_HEADS_GLA * GLA_DK
GLA_WIDTH = N_HEADS_GLA * GLA_DV
GLA_RANK = 16
GLA_TAU = 16.0
GLA_CHUNK = 64
DILATED_PATTERNS = ((128, 1), (512, 4), (2048, 16))
ROT_DIM = HEAD_DIM // 4
ROPE_THETA = 500000.0
D_FF = 2816
EPS = 1e-6
NEG_INF = -1e30
LOG2_E = 1.4426950408889634
PROJ_WIDTH = 3 * DIL_WIDTH + 2 * GLA_KEY_WIDTH + 2 * GLA_WIDTH + 2 * GLA_RANK
QKV_WIDTH = 3 * DIL_WIDTH

LANES = 128
BAND_HALF = 64
Q_TILE = 128
K_WINDOW = Q_TILE + 2 * BAND_HALF
DIL_STEP = 4
VMEM_LIMIT = 56 * 1024 * 1024

F32 = jnp.float32
BF16 = jnp.bfloat16


def _rms(x, g):
    return x * lax.rsqrt(jnp.mean(x * x, axis=-1, keepdims=True) + EPS) * g


def _inproj_kernel(x_ref, g_ref, w_ref, wg_ref, bg_ref, cos_ref, sa_ref, sb_ref,
                   nat_ref, d4_ref, d16_ref, gq_ref, gk_ref, gv_ref, gr_ref, la_ref,
                   slab, slab4):
    TM = x_ref.shape[1]
    R4 = TM // DIL_STEP
    R16 = R4 // DIL_STEP
    h = _rms(x_ref[0], g_ref[...]).astype(BF16)

    def proj(lo, hi):
        return jnp.dot(h, w_ref[:, lo:hi], preferred_element_type=F32)

    cos_t, sin_a, sin_b = cos_ref[...], sa_ref[...], sb_ref[...]

    def emit(yb, blk):
        cs = slice(blk * LANES, (blk + 1) * LANES)
        nat_ref[0, :, cs] = yb.astype(BF16)
        slab[blk] = yb
        for r in range(DIL_STEP):
            slab4[blk, r * R4:(r + 1) * R4, :] = slab[blk, pl.ds(r, R4, stride=DIL_STEP), :]
        for r in range(DIL_STEP):
            d4_ref[0, r, :, cs] = slab4[blk, r * R4:(r + 1) * R4, :].astype(BF16)
            for j in range(DIL_STEP):
                d16_ref[0, r + DIL_STEP * j, :, cs] = slab4[
                    blk, pl.ds(r * R4 + j, R16, stride=DIL_STEP), :].astype(BF16)

    def rotary(xb):
        return (xb * cos_t + pltpu.roll(xb, LANES - ROT_DIM // 2, 1) * sin_a
                + pltpu.roll(xb, ROT_DIM // 2, 1) * sin_b)

    o = PROJ_WIDTH - 2 * GLA_RANK
    g_lr = proj(o, o + 2 * GLA_RANK).astype(BF16)
    pre = jnp.dot(g_lr, wg_ref[...], preferred_element_type=F32) + bg_ref[...]
    la_ref[0] = (jnp.minimum(pre, 0.0) - jnp.log(1.0 + jnp.exp(-jnp.abs(pre)))) * (1.0 / GLA_TAU)

    nblk = DIL_WIDTH // LANES
    t = proj(0, DIL_WIDTH)
    for c in range(nblk):
        emit(rotary(t[:, c * LANES:(c + 1) * LANES]) * (HEAD_DIM ** -0.5 * LOG2_E), c)
    t = proj(DIL_WIDTH, 2 * DIL_WIDTH)
    for c in range(nblk):
        emit(rotary(t[:, c * LANES:(c + 1) * LANES]), nblk + c)
    t = proj(2 * DIL_WIDTH, 3 * DIL_WIDTH)
    for c in range(nblk):
        emit(t[:, c * LANES:(c + 1) * LANES], 2 * nblk + c)
    o = QKV_WIDTH
    gq_ref[0] = (proj(o, o + GLA_KEY_WIDTH) * (GLA_DK ** -0.5)).astype(BF16)
    o += GLA_KEY_WIDTH
    gk_ref[0] = proj(o, o + GLA_KEY_WIDTH).astype(BF16)
    o += GLA_KEY_WIDTH
    gv_ref[0] = proj(o, o + GLA_WIDTH).astype(BF16)
    o += GLA_WIDTH
    gr_ref[0] = proj(o, o + GLA_WIDTH).astype(BF16)


def _rotary_tables(S):
    half = ROT_DIM // 2
    inv_freq = jnp.power(jnp.float32(ROPE_THETA), -jnp.arange(0, ROT_DIM, 2, dtype=F32) / ROT_DIM)
    ang = jnp.arange(S, dtype=F32)[:, None] * inv_freq[None, :]
    cos, sin = jnp.cos(ang), jnp.sin(ang)
    ones = jnp.ones((S, HEAD_DIM - ROT_DIM), F32)
    zeros = jnp.zeros((S, HEAD_DIM - ROT_DIM), F32)
    zh = jnp.zeros((S, half), F32)
    cos_t = jnp.concatenate([cos, cos, ones], axis=1)
    sin_a = jnp.concatenate([-sin, zh, zeros], axis=1)
    sin_b = jnp.concatenate([zh, sin, zeros], axis=1)
    rep = LANES // HEAD_DIM
    return tuple(jnp.tile(t, (1, rep)) for t in (cos_t, sin_a, sin_b))


def _inproj(x, g_mix, w_in, w_gate, b_gate, tm):
    B, S, D = x.shape
    d4, d16 = DIL_STEP, DIL_STEP * DIL_STEP
    cos_t, sin_a, sin_b = _rotary_tables(S)
    tok = lambda w: pl.BlockSpec((1, tm, w), lambda b, i: (b, i, 0))
    res = lambda d: pl.BlockSpec((1, d, tm // d, QKV_WIDTH), lambda b, i: (b, 0, i, 0))
    full = lambda a: pl.BlockSpec(a.shape, lambda b, i: (0,) * a.ndim)
    tab = pl.BlockSpec((tm, LANES), lambda b, i: (i, 0))
    out_shape = (
        jax.ShapeDtypeStruct((B, S, QKV_WIDTH), BF16),
        jax.ShapeDtypeStruct((B, d4, S // d4, QKV_WIDTH), BF16),
        jax.ShapeDtypeStruct((B, d16, S // d16, QKV_WIDTH), BF16),
        jax.ShapeDtypeStruct((B, S, GLA_KEY_WIDTH), BF16),
        jax.ShapeDtypeStruct((B, S, GLA_KEY_WIDTH), BF16),
        jax.ShapeDtypeStruct((B, S, GLA_WIDTH), BF16),
        jax.ShapeDtypeStruct((B, S, GLA_WIDTH), BF16),
        jax.ShapeDtypeStruct((B, S, 2 * GLA_KEY_WIDTH), F32),
    )
    out_specs = (tok(QKV_WIDTH), res(d4), res(d16)) + tuple(tok(s.shape[-1]) for s in out_shape[3:])
    nslab = QKV_WIDTH // LANES
    return pl.pallas_call(
        _inproj_kernel,
        grid=(B, S // tm),
        in_specs=[tok(D), full(g_mix), full(w_in), full(w_gate), full(b_gate), tab, tab, tab],
        out_specs=out_specs,
        out_shape=out_shape,
        scratch_shapes=[pltpu.VMEM((nslab, tm, LANES), F32)] * 2,
        compiler_params=pltpu.CompilerParams(
            dimension_semantics=("parallel", "parallel"), vmem_limit_bytes=VMEM_LIMIT),
        name="inproj",
    )(x, g_mix, w_in, w_gate, b_gate, cos_t, sin_a, sin_b)


def _dil_kernel(*refs, L, pos_axis, has_prev, final):
    q_ref, kl_ref, km_ref, kr_ref, vl_ref, vm_ref, vr_ref = refs[:7]
    refs = refs[7:]
    if has_prev:
        op_ref, lp_ref = refs[:2]
        refs = refs[2:]
    if final:
        o_ref, kwin, vwin, so = refs
    else:
        o_ref, l_ref, kwin, vwin, so, sl = refs
    NSEQ, TL, CW = q_ref.shape
    i = pl.program_id(pos_axis)

    kwin[:, 0:BAND_HALF] = kl_ref[...]
    kwin[:, BAND_HALF:BAND_HALF + TL] = km_ref[...]
    kwin[:, BAND_HALF + TL:] = kr_ref[...]
    vwin[:, 0:BAND_HALF] = vl_ref[...]
    vwin[:, BAND_HALF:BAND_HALF + TL] = vm_ref[...]
    vwin[:, BAND_HALF + TL:] = vr_ref[...]

    lane = lax.broadcasted_iota(jnp.int32, (1, LANES), 1)
    head0 = lane < HEAD_DIM
    m0 = head0.astype(BF16)
    m1 = (1.0 - head0.astype(F32)).astype(BF16)
    row = lax.broadcasted_iota(jnp.int32, (Q_TILE, K_WINDOW), 0)
    col = lax.broadcasted_iota(jnp.int32, (Q_TILE, K_WINDOW), 1)
    band = (col >= row) & (col <= row + 2 * BAND_HALF)
    ones_v = jnp.ones((K_WINDOW, LANES), BF16)

    for j in range(TL // Q_TILE):
        r0 = j * Q_TILE
        kpos = i * TL + (r0 - BAND_HALF) + col
        valid = band & (kpos >= 0) & (kpos < L)
        bias = jnp.where(valid, 0.0, NEG_INF)
        bias2 = jnp.concatenate([bias, bias], axis=0)
        for n in range(NSEQ):
            for hp in range(CW // LANES):
                cs = slice(hp * LANES, (hp + 1) * LANES)
                q = q_ref[n, r0:r0 + Q_TILE, cs]
                k = kwin[n, r0:r0 + K_WINDOW, cs]
                v = vwin[n, r0:r0 + K_WINDOW, cs]
                qq = jnp.concatenate([q * m0, q * m1], axis=0)
                s = lax.dot_general(qq, k, (((1,), (1,)), ((), ())),
                                    preferred_element_type=F32) + bias2
                m = jnp.max(s, axis=1, keepdims=True)
                p = jnp.exp2(s - m).astype(BF16)
                pv = jnp.dot(p, jnp.concatenate([v, ones_v], axis=1),
                             preferred_element_type=F32)
                acc = jnp.where(head0, pv[:Q_TILE, :LANES], pv[Q_TILE:, :LANES])
                ll = jnp.where(head0, pv[:Q_TILE, LANES:], pv[Q_TILE:, LANES:])
                mm = jnp.where(head0, m[:Q_TILE], m[Q_TILE:])
                if has_prev:
                    op = op_ref[n, r0:r0 + Q_TILE, cs].astype(F32)
                    lp = lp_ref[n, r0:r0 + Q_TILE, cs]
                    mn = jnp.maximum(lp, mm)
                    w1 = jnp.exp2(lp - mn)
                    w2 = jnp.exp2(mm - mn)
                    den = w1 + ll * w2
                    o = (op * w1 + acc * w2) / den
                    lse = mn + jnp.log2(den)
                else:
                    o = acc / ll
                    lse = mm + jnp.log2(ll)
                if final:
                    so[hp, pl.ds(n + NSEQ * r0, Q_TILE, stride=NSEQ), :] = o
                else:
                    so[hp, r0:r0 + Q_TILE, :] = o
                    sl[hp, r0:r0 + Q_TILE, :] = lse

    for hp in range(CW // LANES):
        cs = slice(hp * LANES, (hp + 1) * LANES)
        if final:
            o_ref[:, cs] = so[hp].astype(o_ref.dtype)
        else:
            TQ = TL // DIL_STEP
            for r in range(DIL_STEP):
                o_ref[r, :, cs] = so[hp, pl.ds(r, TQ, stride=DIL_STEP), :].astype(o_ref.dtype)
                l_ref[r, :, cs] = sl[hp, pl.ds(r, TQ, stride=DIL_STEP), :]


def _dil_call(qkv, prev, L, grid, blk, in_idx, halo_l, halo_r, state_spec, out_specs, out_shape,
              pos_axis, final, name):
    NSEQ, TL, CW = blk
    main = lambda t: pl.BlockSpec((None, NSEQ, TL, CW), functools.partial(in_idx, t=t))
    left = lambda t: pl.BlockSpec((None, NSEQ, BAND_HALF, CW), functools.partial(halo_l, t=t))
    right = lambda t: pl.BlockSpec((None, NSEQ, BAND_HALF, CW), functools.partial(halo_r, t=t))
    in_specs = [main(0), left(1), main(1), right(1), left(2), main(2), right(2)]
    args = [qkv] * 7
    if prev is not None:
        in_specs += [state_spec, state_spec]
        args += list(prev)
    nhp = CW // LANES
    scratch = [pltpu.VMEM((NSEQ, TL + 2 * BAND_HALF, CW), BF16)] * 2
    scratch += [pltpu.VMEM((nhp, NSEQ * TL, LANES), F32)] * (1 if final else 2)
    return pl.pallas_call(
        functools.partial(_dil_kernel, L=L, pos_axis=pos_axis, has_prev=prev is not None, final=final),
        grid=grid,
        in_specs=in_specs,
        out_specs=out_specs,
        out_shape=out_shape,
        scratch_shapes=scratch,
        compiler_params=pltpu.CompilerParams(
            dimension_semantics=("parallel",) * len(grid), vmem_limit_bytes=VMEM_LIMIT),
        name=name,
    )(*args)


def _dilated_mixture(qkv_nat, qkv_d4, qkv_d16):
    B, S, _ = qkv_nat.shape
    W = DIL_WIDTH
    d4, d16 = DIL_STEP, DIL_STEP * DIL_STEP
    L4, L16 = S // d4, S // d16
    TL = 1024
    nb = TL // BAND_HALF

    st4 = lambda dt: jax.ShapeDtypeStruct((B, d4, L4, W), dt)
    spec4 = pl.BlockSpec((None, d4, TL // d4, W), lambda b, i: (b, 0, i, 0))
    state = _dil_call(
        qkv_nat.reshape(B, 1, S, QKV_WIDTH), None, S, (B, S // TL), (1, TL, W),
        lambda b, i, t: (b, 0, i, t),
        lambda b, i, t: (b, 0, jnp.maximum(i * nb - 1, 0), t),
        lambda b, i, t: (b, 0, jnp.minimum((i + 1) * nb, S // BAND_HALF - 1), t),
        None, (spec4, spec4), (st4(BF16), st4(F32)), pos_axis=1, final=False, name="dilated_d1")

    st16 = lambda dt: jax.ShapeDtypeStruct((B, d4, d4, L16, W), dt)
    spec16 = pl.BlockSpec((None, d4, None, TL // d4, W), lambda b, r, i: (b, 0, r, i, 0))
    state = _dil_call(
        qkv_d4, state, L4, (B, d4, L4 // TL), (1, TL, W),
        lambda b, r, i, t: (b, r, i, t),
        lambda b, r, i, t: (b, r, jnp.maximum(i * nb - 1, 0), t),
        lambda b, r, i, t: (b, r, jnp.minimum((i + 1) * nb, L4 // BAND_HALF - 1), t),
        pl.BlockSpec((None, 1, TL, W), lambda b, r, i: (b, r, i, 0)),
        (spec16, spec16), (st16(BF16), st16(F32)), pos_axis=2, final=False, name="dilated_d4")
    state = tuple(s.reshape(B, d16, L16, W) for s in state)

    CW = LANES
    TL16 = 2 * Q_TILE
    nbq = TL16 // BAND_HALF
    span = d16 * TL16
    return _dil_call(
        qkv_d16, state, L16, (B, S // span, W // CW), (d16, TL16, CW),
        lambda b, i, c, t: (b, 0, i, t * (W // CW) + c),
        lambda b, i, c, t: (b, 0, jnp.maximum(i * nbq - 1, 0), t * (W // CW) + c),
        lambda b, i, c, t: (b, 0, jnp.minimum((i + 1) * nbq, L16 // BAND_HALF - 1), t * (W // CW) + c),
        pl.BlockSpec((None, d16, TL16, CW), lambda b, i, c: (b, 0, i, c)),
        pl.BlockSpec((None, span, CW), lambda b, i, c: (b, i, c)),
        jax.ShapeDtypeStruct((B, S, W), BF16), pos_axis=1, final=True, name="dilated_d16")


def _gla_kernel(*refs, TS, forward):
    if forward:
        q_ref, k_ref, la_ref, v_ref, ob_ref, r_ref, g_ref, o_ref, state_ref = refs
    else:
        q_ref, k_ref, la_ref, v_ref, o_ref, state_ref = refs
    C = GLA_CHUNK
    nch = TS // C

    @pl.when(pl.program_id(2) == 0)
    def _():
        state_ref[...] = jnp.zeros_like(state_ref)

    lane = lax.broadcasted_iota(jnp.int32, (1, LANES), 1)
    head0 = (lane < GLA_DK)[None]
    tr = lax.broadcasted_iota(jnp.int32, (C, C), 0)
    tc = lax.broadcasted_iota(jnp.int32, (C, C), 1)
    keep = (tc <= tr) if forward else (tc > tr)
    keep2 = jnp.concatenate([keep, keep], axis=0)[None]

    b = la_ref[0] * LOG2_E
    rowc = lax.broadcasted_iota(jnp.int32, (TS, LANES), 0) % C
    step = 1
    while step < C:
        if forward:
            b = b + jnp.where(rowc >= step, pltpu.roll(b, step, 0), 0.0)
        else:
            b = b + jnp.where(rowc < C - step, pltpu.roll(b, TS - step, 0), 0.0)
        step *= 2
    b = b.reshape(nch, C, LANES)
    total = b[:, C - 1:C, :] if forward else b[:, 0:1, :]

    q = q_ref[0].reshape(nch, C, LANES)
    k = k_ref[0].reshape(nch, C, LANES)
    v = v_ref[0].reshape(nch, C, 2 * GLA_DV)
    decay = jnp.exp2(total)
    q_in = (q * jnp.exp2(b)).astype(BF16)
    k_inf = k * jnp.exp2(-b)
    k_in = k_inf.astype(BF16)
    k_dec = (k_inf * decay).astype(BF16)
    zero = jnp.zeros_like(q_in)
    qs = jnp.concatenate([jnp.where(head0, q_in, zero), jnp.where(head0, zero, q_in)], axis=1)
    att = jnp.einsum("cqd,ckd->cqk", qs, k_in, preferred_element_type=F32)
    att = jnp.where(keep2, att, 0.0).astype(BF16)
    intra = jnp.einsum("cqk,ckv->cqv", att, v, preferred_element_type=F32)
    ks = jnp.concatenate([jnp.where(head0, k_dec, zero), jnp.where(head0, zero, k_dec)], axis=1)
    vs = jnp.concatenate([v[:, :, :GLA_DV], v[:, :, GLA_DV:]], axis=1)
    kv_t = jnp.einsum("ctv,ctl->cvl", vs, ks, preferred_element_type=F32)

    state_t = state_ref[...]
    states = [None] * nch
    for c in (range(nch) if forward else range(nch - 1, -1, -1)):
        states[c] = state_t.astype(BF16)
        state_t = state_t * decay[c] + kv_t[c]
    state_ref[...] = state_t
    inter = jnp.einsum("cql,cvl->cqv", qs, jnp.stack(states), preferred_element_type=F32)
    o = jnp.concatenate([intra[:, :C, :GLA_DV] + inter[:, :C], intra[:, C:, GLA_DV:] + inter[:, C:]],
                        axis=2).reshape(TS, 2 * GLA_DV)
    if forward:
        tot = o + ob_ref[0].astype(F32)
        r = r_ref[0].astype(F32)
        gate = r / (1.0 + jnp.exp(-r))
        for hh in range(2):
            hs = slice(hh * GLA_DV, (hh + 1) * GLA_DV)
            o_ref[0, :, hs] = (_rms(tot[:, hs], g_ref[...]) * gate[:, hs]).astype(o_ref.dtype)
    else:
        o_ref[0] = o.astype(o_ref.dtype)


def _gla_pass(gq, gk, la, gv, ts, forward, ob=None, gr=None, g_gla=None):
    B, S, _ = gq.shape
    nt = S // ts
    tile = (lambda i: i) if forward else (lambda i: nt - 1 - i)
    la_off = 0 if forward else GLA_KEY_WIDTH // LANES
    key = pl.BlockSpec((1, ts, LANES), lambda b, h, i: (b, tile(i), h))
    gate = pl.BlockSpec((1, ts, LANES), lambda b, h, i: (b, tile(i), la_off + h))
    val = pl.BlockSpec((1, ts, 2 * GLA_DV), lambda b, h, i: (b, tile(i), h))
    in_specs = [key, key, gate, val]
    args = [gq, gk, la, gv]
    if forward:
        in_specs += [val, val, pl.BlockSpec(g_gla.shape, lambda b, h, i: (0, 0))]
        args += [ob, gr, g_gla]
    return pl.pallas_call(
        functools.partial(_gla_kernel, TS=ts, forward=forward),
        grid=(B, N_HEADS_GLA // 2, nt),
        in_specs=in_specs,
        out_specs=val,
        out_shape=jax.ShapeDtypeStruct((B, S, GLA_WIDTH), BF16),
        scratch_shapes=[pltpu.VMEM((GLA_DV, 2 * GLA_DK), F32)],
        compiler_params=pltpu.CompilerParams(
            dimension_semantics=("parallel", "parallel", "arbitrary"), vmem_limit_bytes=VMEM_LIMIT),
        name="gla_fwd" if forward else "gla_bwd",
    )(*args)


def _tail_kernel(x_ref, od_ref, og_ref, wo_ref, gf_ref, wg_ref, wu_ref, wd_ref, gl_ref, y_ref, *, ff_chunk):
    mix = (jnp.dot(od_ref[...], wo_ref[:DIL_WIDTH, :], preferred_element_type=F32)
           + jnp.dot(og_ref[...], wo_ref[DIL_WIDTH:, :], preferred_element_type=F32))
    x1 = x_ref[...] + mix
    h = _rms(x1, gf_ref[...]).astype(BF16)
    acc = x1
    for c0 in range(0, D_FF, ff_chunk):
        gate = jnp.dot(h, wg_ref[:, c0:c0 + ff_chunk], preferred_element_type=F32)
        up = jnp.dot(h, wu_ref[:, c0:c0 + ff_chunk], preferred_element_type=F32)
        act = (gate / (1.0 + jnp.exp(-gate)) * up).astype(BF16)
        acc = acc + jnp.dot(act, wd_ref[c0:c0 + ff_chunk, :], preferred_element_type=F32)
    y_ref[...] = _rms(acc, gl_ref[...])


def _tail(x, o_dil, o_gla, w_out, g_ffn, w_gate, w_up, w_down, g_final, tm, ff_chunk):
    N, D = x.shape
    tok = lambda w: pl.BlockSpec((tm, w), lambda i: (i, 0))
    full = lambda a: pl.BlockSpec(a.shape, lambda i: (0, 0), pipeline_mode=pl.Buffered(1))
    return pl.pallas_call(
        functools.partial(_tail_kernel, ff_chunk=ff_chunk),
        grid=(N // tm,),
        in_specs=[tok(D), tok(DIL_WIDTH), tok(GLA_WIDTH), full(w_out), full(g_ffn),
                  full(w_gate), full(w_up), full(w_down), full(g_final)],
        out_specs=tok(D),
        out_shape=jax.ShapeDtypeStruct((N, D), F32),
        compiler_params=pltpu.CompilerParams(
            dimension_semantics=("parallel",), vmem_limit_bytes=VMEM_LIMIT),
        name="tail",
    )(x, o_dil, o_gla, w_out, g_ffn, w_gate, w_up, w_down, g_final)


def _trunk(x, p, tm_in=512, ts_gla=2048, tm_tail=1024, ff_chunk=256):
    B, S, D = x.shape
    qkv_nat, qkv_d4, qkv_d16, gq, gk, gv, gr, la = _inproj(
        x, p["g_mix"], p["w_in"], p["w_gate"], p["b_gate"], tm_in)
    o_dil = _dilated_mixture(qkv_nat, qkv_d4, qkv_d16)
    ob = _gla_pass(gq, gk, la, gv, ts_gla, forward=False)
    o_gla = _gla_pass(gq, gk, la, gv, ts_gla, forward=True, ob=ob, gr=gr, g_gla=p["g_gla"])
    y = _tail(x.reshape(B * S, D), o_dil.reshape(B * S, DIL_WIDTH), o_gla.reshape(B * S, GLA_WIDTH),
              p["w_out"], p["g_ffn"], p["w_ffn_gate"], p["w_ffn_up"], p["w_ffn_down"], p["g_final"],
              tm_tail, ff_chunk)
    return y.reshape(B, S, D)


def _prepare_params(g_mix, w_in, w_gate_f, b_gate_f, w_gate_b, b_gate_b, g_gla, w_out,
                    g_ffn, w_ffn_gate, w_ffn_up, w_ffn_down, g_final):
    z = jnp.zeros_like(w_gate_f[0])
    w_gate = jnp.concatenate(
        [jnp.concatenate([w_gate_f[0], z], axis=1), jnp.concatenate([z, w_gate_b[0]], axis=1)], axis=0)
    return {
        "g_mix": g_mix[0][None, :],
        "w_in": w_in[0].astype(BF16),
        "w_gate": w_gate.astype(BF16),
        "b_gate": jnp.concatenate([b_gate_f[0], b_gate_b[0]])[None, :],
        "g_gla": g_gla[0][None, :],
        "w_out": w_out[0].astype(BF16),
        "g_ffn": g_ffn[0][None, :],
        "w_ffn_gate": w_ffn_gate[0].astype(BF16),
        "w_ffn_up": w_ffn_up[0].astype(BF16),
        "w_ffn_down": w_ffn_down[0].astype(BF16),
        "g_final": g_final[None, :],
    }


def kernel(x_prompt, x_sample, g_mix, w_in, w_gate_f, b_gate_f, w_gate_b, b_gate_b, g_gla, w_out,
           g_ffn, w_ffn_gate, w_ffn_up, w_ffn_down, g_final):
    p = _prepare_params(g_mix, w_in, w_gate_f, b_gate_f, w_gate_b, b_gate_b, g_gla, w_out,
                        g_ffn, w_ffn_gate, w_ffn_up, w_ffn_down, g_final)
    return (_trunk(x_prompt, p), _trunk(x_sample, p))
```

```python
import functools

import jax
import jax.numpy as jnp
from jax import lax
from jax.experimental import pallas as pl
from jax.experimental.pallas import tpu as pltpu

D_MODEL = 1024
HEAD_DIM = 64
N_HEADS_DIL = 8
DIL_WIDTH = N_HEADS_DIL * HEAD_DIM
N_HEADS_GLA = 4
GLA_DK = 64
GLA_DV = 128
GLA_KEY_WIDTH = N_HEADS_GLA * GLA_DK
GLA_WIDTH = N_HEADS_GLA * GLA_DV
GLA_RANK = 16
GLA_TAU = 16.0
GLA_CHUNK = 64
DILATED_PATTERNS = ((128, 1), (512, 4), (2048, 16))
ROT_DIM = HEAD_DIM // 4
ROPE_THETA = 500000.0
D_FF = 2816
EPS = 1e-6
NEG_INF = -1e30
LOG2_E = 1.4426950408889634
PROJ_WIDTH = 3 * DIL_WIDTH + 2 * GLA_KEY_WIDTH + 2 * GLA_WIDTH + 2 * GLA_RANK
QKV_WIDTH = 3 * DIL_WIDTH

LANES = 128
BAND_HALF = 64
Q_TILE = 128
K_WINDOW = Q_TILE + 2 * BAND_HALF
DIL_STEP = 4
VMEM_LIMIT = 56 * 1024 * 1024

F32 = jnp.float32
BF16 = jnp.bfloat16


def _rms(x, g):
    return x * lax.rsqrt(jnp.mean(x * x, axis=-1, keepdims=True) + EPS) * g


def _inproj_kernel(x_ref, g_ref, w_ref, wg_ref, bg_ref, cos_ref, sa_ref, sb_ref,
                   nat_ref, d4_ref, d16_ref, gq_ref, gk_ref, gv_ref, gr_ref, la_ref,
                   slab, slab4):
    TM = x_ref.shape[1]
    R4 = TM // DIL_STEP
    R16 = R4 // DIL_STEP
    h = _rms(x_ref[0], g_ref[...]).astype(BF16)

    def proj(lo, hi):
        return jnp.dot(h, w_ref[:, lo:hi], preferred_element_type=F32)

    cos_t, sin_a, sin_b = cos_ref[...], sa_ref[...], sb_ref[...]

    def emit(yb, blk):
        cs = slice(blk * LANES, (blk + 1) * LANES)
        nat_ref[0, :, cs] = yb.astype(BF16)
        slab[...] = yb
        for r in range(DIL_STEP):
            slab4[r * R4:(r + 1) * R4, :] = slab[pl.ds(r, R4, stride=DIL_STEP), :]
        for r in range(DIL_STEP):
            d4_ref[0, r, :, cs] = slab4[r * R4:(r + 1) * R4, :].astype(BF16)
            for j in range(DIL_STEP):
                d16_ref[0, r + DIL_STEP * j, :, cs] = slab4[
                    pl.ds(r * R4 + j, R16, stride=DIL_STEP), :].astype(BF16)

    def rotary(xb):
        return (xb * cos_t + pltpu.roll(xb, LANES - ROT_DIM // 2, 1) * sin_a
                + pltpu.roll(xb, ROT_DIM // 2, 1) * sin_b)

    o = PROJ_WIDTH - 2 * GLA_RANK
    g_lr = proj(o, o + 2 * GLA_RANK).astype(BF16)
    pre = jnp.dot(g_lr, wg_ref[...], preferred_element_type=F32) + bg_ref[...]
    la_ref[0] = (jnp.minimum(pre, 0.0) - jnp.log(1.0 + jnp.exp(-jnp.abs(pre)))) * (1.0 / GLA_TAU)

    nblk = DIL_WIDTH // LANES
    t = proj(0, DIL_WIDTH)
    for c in range(nblk):
        emit(rotary(t[:, c * LANES:(c + 1) * LANES]) * (HEAD_DIM ** -0.5 * LOG2_E), c)
    t = proj(DIL_WIDTH, 2 * DIL_WIDTH)
    for c in range(nblk):
        emit(rotary(t[:, c * LANES:(c + 1) * LANES]), nblk + c)
    t = proj(2 * DIL_WIDTH, 3 * DIL_WIDTH)
    for c in range(nblk):
        emit(t[:, c * LANES:(c + 1) * LANES], 2 * nblk + c)
    o = QKV_WIDTH
    gq_ref[0] = (proj(o, o + GLA_KEY_WIDTH) * (GLA_DK ** -0.5)).astype(BF16)
    o += GLA_KEY_WIDTH
    gk_ref[0] = proj(o, o + GLA_KEY_WIDTH).astype(BF16)
    o += GLA_KEY_WIDTH
    gv_ref[0] = proj(o, o + GLA_WIDTH).astype(BF16)
    o += GLA_WIDTH
    gr_ref[0] = proj(o, o + GLA_WIDTH).astype(BF16)


def _rotary_tables(S):
    half = ROT_DIM // 2
    inv_freq = jnp.power(jnp.float32(ROPE_THETA), -jnp.arange(0, ROT_DIM, 2, dtype=F32) / ROT_DIM)
    ang = jnp.arange(S, dtype=F32)[:, None] * inv_freq[None, :]
    cos, sin = jnp.cos(ang), jnp.sin(ang)
    ones = jnp.ones((S, HEAD_DIM - ROT_DIM), F32)
    zeros = jnp.zeros((S, HEAD_DIM - ROT_DIM), F32)
    zh = jnp.zeros((S, half), F32)
    cos_t = jnp.concatenate([cos, cos, ones], axis=1)
    sin_a = jnp.concatenate([-sin, zh, zeros], axis=1)
    sin_b = jnp.concatenate([zh, sin, zeros], axis=1)
    rep = LANES // HEAD_DIM
    return tuple(jnp.tile(t, (1, rep)) for t in (cos_t, sin_a, sin_b))


def _inproj(x, g_mix, w_in, w_gate, b_gate, tm):
    B, S, D = x.shape
    d4, d16 = DIL_STEP, DIL_STEP * DIL_STEP
    cos_t, sin_a, sin_b = _rotary_tables(S)
    tok = lambda w: pl.BlockSpec((1, tm, w), lambda b, i: (b, i, 0))
    res = lambda d: pl.BlockSpec((1, d, tm // d, QKV_WIDTH), lambda b, i: (b, 0, i, 0))
    full = lambda a: pl.BlockSpec(a.shape, lambda b, i: (0,) * a.ndim, pipeline_mode=pl.Buffered(1))
    tab = pl.BlockSpec((tm, LANES), lambda b, i: (i, 0))
    out_shape = (
        jax.ShapeDtypeStruct((B, S, QKV_WIDTH), BF16),
        jax.ShapeDtypeStruct((B, d4, S // d4, QKV_WIDTH), BF16),
        jax.ShapeDtypeStruct((B, d16, S // d16, QKV_WIDTH), BF16),
        jax.ShapeDtypeStruct((B, S, GLA_KEY_WIDTH), BF16),
        jax.ShapeDtypeStruct((B, S, GLA_KEY_WIDTH), BF16),
        jax.ShapeDtypeStruct((B, S, GLA_WIDTH), BF16),
        jax.ShapeDtypeStruct((B, S, GLA_WIDTH), BF16),
        jax.ShapeDtypeStruct((B, S, 2 * GLA_KEY_WIDTH), F32),
    )
    out_specs = (tok(QKV_WIDTH), res(d4), res(d16)) + tuple(tok(s.shape[-1]) for s in out_shape[3:])
    return pl.pallas_call(
        _inproj_kernel,
        grid=(B, S // tm),
        in_specs=[tok(D), full(g_mix), full(w_in), full(w_gate), full(b_gate), tab, tab, tab],
        out_specs=out_specs,
        out_shape=out_shape,
        scratch_shapes=[pltpu.VMEM((tm, LANES), F32)] * 2,
        compiler_params=pltpu.CompilerParams(
            dimension_semantics=("parallel", "parallel"), vmem_limit_bytes=VMEM_LIMIT),
        name="inproj",
    )(x, g_mix, w_in, w_gate, b_gate, cos_t, sin_a, sin_b)


def _dil_kernel(*refs, L, pos_axis, has_prev, final):
    q_ref, kl_ref, km_ref, kr_ref, vl_ref, vm_ref, vr_ref = refs[:7]
    refs = refs[7:]
    if has_prev:
        op_ref, lp_ref = refs[:2]
        refs = refs[2:]
    if final:
        o_ref, kwin, vwin, so = refs
    else:
        o_ref, l_ref, kwin, vwin, so, sl = refs
    NSEQ, TL, CW = q_ref.shape
    i = pl.program_id(pos_axis)

    kwin[:, 0:BAND_HALF] = kl_ref[...]
    kwin[:, BAND_HALF:BAND_HALF + TL] = km_ref[...]
    kwin[:, BAND_HALF + TL:] = kr_ref[...]
    vwin[:, 0:BAND_HALF] = vl_ref[...]
    vwin[:, BAND_HALF:BAND_HALF + TL] = vm_ref[...]
    vwin[:, BAND_HALF + TL:] = vr_ref[...]

    lane = lax.broadcasted_iota(jnp.int32, (1, LANES), 1)
    head0 = lane < HEAD_DIM
    m0 = head0.astype(BF16)
    m1 = (1.0 - head0.astype(F32)).astype(BF16)
    row = lax.broadcasted_iota(jnp.int32, (Q_TILE, K_WINDOW), 0)
    col = lax.broadcasted_iota(jnp.int32, (Q_TILE, K_WINDOW), 1)
    band = (col >= row) & (col <= row + 2 * BAND_HALF)
    ones_v = jnp.ones((K_WINDOW, LANES), BF16)

    for j in range(TL // Q_TILE):
        r0 = j * Q_TILE
        kpos = i * TL + (r0 - BAND_HALF) + col
        valid = band & (kpos >= 0) & (kpos < L)
        bias = jnp.where(valid, 0.0, NEG_INF)
        bias2 = jnp.concatenate([bias, bias], axis=0)
        for n in range(NSEQ):
            for hp in range(CW // LANES):
                cs = slice(hp * LANES, (hp + 1) * LANES)
                q = q_ref[n, r0:r0 + Q_TILE, cs]
                k = kwin[n, r0:r0 + K_WINDOW, cs]
                v = vwin[n, r0:r0 + K_WINDOW, cs]
                qq = jnp.concatenate([q * m0, q * m1], axis=0)
                s = lax.dot_general(qq, k, (((1,), (1,)), ((), ())),
                                    preferred_element_type=F32) + bias2
                m = jnp.max(s, axis=1, keepdims=True)
                p = jnp.exp2(s - m).astype(BF16)
                pv = jnp.dot(p, jnp.concatenate([v, ones_v], axis=1),
                             preferred_element_type=F32)
                acc = jnp.where(head0, pv[:Q_TILE, :LANES], pv[Q_TILE:, :LANES])
                ll = jnp.where(head0, pv[:Q_TILE, LANES:], pv[Q_TILE:, LANES:])
                mm = jnp.where(head0, m[:Q_TILE], m[Q_TILE:])
                if has_prev:
                    op = op_ref[n, r0:r0 + Q_TILE, cs].astype(F32)
                    lp = lp_ref[n, r0:r0 + Q_TILE, cs]
                    mn = jnp.maximum(lp, mm)
                    w1 = jnp.exp2(lp - mn)
                    w2 = jnp.exp2(mm - mn)
                    den = w1 + ll * w2
                    o = (op * w1 + acc * w2) / den
                    lse = mn + jnp.log2(den)
                else:
                    o = acc / ll
                    lse = mm + jnp.log2(ll)
                if final:
                    so[hp, pl.ds(n + NSEQ * r0, Q_TILE, stride=NSEQ), :] = o
                else:
                    so[hp, r0:r0 + Q_TILE, :] = o
                    sl[hp, r0:r0 + Q_TILE, :] = lse

    for hp in range(CW // LANES):
        cs = slice(hp * LANES, (hp + 1) * LANES)
        if final:
            o_ref[:, cs] = so[hp].astype(o_ref.dtype)
        else:
            TQ = TL // DIL_STEP
            for r in range(DIL_STEP):
                o_ref[r, :, cs] = so[hp, pl.ds(r, TQ, stride=DIL_STEP), :].astype(o_ref.dtype)
                l_ref[r, :, cs] = sl[hp, pl.ds(r, TQ, stride=DIL_STEP), :]


def _dil_call(qkv, prev, L, grid, blk, in_idx, halo_l, halo_r, state_spec, out_specs, out_shape,
              pos_axis, final, name):
    NSEQ, TL, CW = blk
    main = lambda t: pl.BlockSpec((None, NSEQ, TL, CW), functools.partial(in_idx, t=t))
    left = lambda t: pl.BlockSpec((None, NSEQ, BAND_HALF, CW), functools.partial(halo_l, t=t))
    right = lambda t: pl.BlockSpec((None, NSEQ, BAND_HALF, CW), functools.partial(halo_r, t=t))
    in_specs = [main(0), left(1), main(1), right(1), left(2), main(2), right(2)]
    args = [qkv] * 7
    if prev is not None:
        in_specs += [state_spec, state_spec]
        args += list(prev)
    nhp = CW // LANES
    scratch = [pltpu.VMEM((NSEQ, TL + 2 * BAND_HALF, CW), BF16)] * 2
    scratch += [pltpu.VMEM((nhp, NSEQ * TL, LANES), F32)] * (1 if final else 2)
    return pl.pallas_call(
        functools.partial(_dil_kernel, L=L, pos_axis=pos_axis, has_prev=prev is not None, final=final),
        grid=grid,
        in_specs=in_specs,
        out_specs=out_specs,
        out_shape=out_shape,
        scratch_shapes=scratch,
        compiler_params=pltpu.CompilerParams(
            dimension_semantics=("parallel",) * len(grid), vmem_limit_bytes=VMEM_LIMIT),
        name=name,
    )(*args)


def _dilated_mixture(qkv_nat, qkv_d4, qkv_d16):
    B, S, _ = qkv_nat.shape
    W = DIL_WIDTH
    d4, d16 = DIL_STEP, DIL_STEP * DIL_STEP
    L4, L16 = S // d4, S // d16
    MAX_ROWS = 2048

    TL = min(MAX_ROWS, S)
    nb = TL // BAND_HALF
    st4 = lambda dt: jax.ShapeDtypeStruct((B, d4, L4, W), dt)
    spec4 = pl.BlockSpec((None, d4, TL // d4, W), lambda b, i: (b, 0, i, 0))
    state = _dil_call(
        qkv_nat.reshape(B, 1, S, QKV_WIDTH), None, S, (B, S // TL), (1, TL, W),
        lambda b, i, t: (b, 0, i, t),
        lambda b, i, t: (b, 0, jnp.maximum(i * nb - 1, 0), t),
        lambda b, i, t: (b, 0, jnp.minimum((i + 1) * nb, S // BAND_HALF - 1), t),
        None, (spec4, spec4), (st4(BF16), st4(F32)), pos_axis=1, final=False, name="dilated_d1")

    TL = min(MAX_ROWS, L4)
    nb = TL // BAND_HALF
    st16 =lambda dt: jax.ShapeDtypeStruct((B, d4, d4, L16, W), dt)
    spec16 = pl.BlockSpec((None, d4, None, TL // d4, W), lambda b, r, i: (b, 0, r, i, 0))
    state = _dil_call(
        qkv_d4, state, L4, (B, d4, L4 // TL), (1, TL, W),
        lambda b, r, i, t: (b, r, i, t),
        lambda b, r, i, t: (b, r, jnp.maximum(i * nb - 1, 0), t),
        lambda b, r, i, t: (b, r, jnp.minimum((i + 1) * nb, L4 // BAND_HALF - 1), t),
        pl.BlockSpec((None, 1, TL, W), lambda b, r, i: (b, r, i, 0)),
        (spec16, spec16), (st16(BF16), st16(F32)), pos_axis=2, final=False, name="dilated_d4")
    state = tuple(s.reshape(B, d16, L16, W) for s in state)

    CW = LANES
    TL16 = 2 * Q_TILE
    nbq = TL16 // BAND_HALF
    span = d16 * TL16
    return _dil_call(
        qkv_d16, state, L16, (B, S // span, W // CW), (d16, TL16, CW),
        lambda b, i, c, t: (b, 0, i, t * (W // CW) + c),
        lambda b, i, c, t: (b, 0, jnp.maximum(i * nbq - 1, 0), t * (W // CW) + c),
        lambda b, i, c, t: (b, 0, jnp.minimum((i + 1) * nbq, L16 // BAND_HALF - 1), t * (W // CW) + c),
        pl.BlockSpec((None, d16, TL16, CW), lambda b, i, c: (b, 0, i, c)),
        pl.BlockSpec((None, span, CW), lambda b, i, c: (b, i, c)),
        jax.ShapeDtypeStruct((B, S, W), BF16), pos_axis=1, final=True, name="dilated_d16")


def _gla_kernel(*refs, TS, forward):
    if forward:
        q_ref, k_ref, la_ref, v_ref, ob_ref, r_ref, g_ref, o_ref, state_ref = refs
    else:
        q_ref, k_ref, la_ref, v_ref, o_ref, state_ref = refs
    C = GLA_CHUNK
    nch = TS // C

    @pl.when(pl.program_id(2) == 0)
    def _():
        state_ref[...] = jnp.zeros_like(state_ref)

    lane = lax.broadcasted_iota(jnp.int32, (1, LANES), 1)
    head0 = (lane < GLA_DK)[None]
    tr = lax.broadcasted_iota(jnp.int32, (C, C), 0)
    tc = lax.broadcasted_iota(jnp.int32, (C, C), 1)
    keep = (tc <= tr) if forward else (tc > tr)
    keep2 = jnp.concatenate([keep, keep], axis=0)[None]

    b = la_ref[0] * LOG2_E
    rowc = lax.broadcasted_iota(jnp.int32, (TS, LANES), 0) % C
    step = 1
    while step < C:
        if forward:
            b = b + jnp.where(rowc >= step, pltpu.roll(b, step, 0), 0.0)
        else:
            b = b + jnp.where(rowc < C - step, pltpu.roll(b, TS - step, 0), 0.0)
        step *= 2
    b = b.reshape(nch, C, LANES)
    total = b[:, C - 1:C, :] if forward else b[:, 0:1, :]

    q = q_ref[0].reshape(nch, C, LANES)
    k = k_ref[0].reshape(nch, C, LANES)
    v = v_ref[0].reshape(nch, C, 2 * GLA_DV)
    decay = jnp.exp2(total)
    q_in = (q * jnp.exp2(b)).astype(BF16)
    k_inf = k * jnp.exp2(-b)
    k_in = k_inf.astype(BF16)
    k_dec = (k_inf * decay).astype(BF16)
    zero = jnp.zeros_like(q_in)
    qs = jnp.concatenate([jnp.where(head0, q_in, zero), jnp.where(head0, zero, q_in)], axis=1)
    att = jnp.einsum("cqd,ckd->cqk", qs, k_in, preferred_element_type=F32)
    att = jnp.where(keep2, att, 0.0).astype(BF16)
    intra = jnp.einsum("cqk,ckv->cqv", att, v, preferred_element_type=F32)
    ks = jnp.concatenate([jnp.where(head0, k_dec, zero), jnp.where(head0, zero, k_dec)], axis=1)
    vs = jnp.concatenate([v[:, :, :GLA_DV], v[:, :, GLA_DV:]], axis=1)
    kv_t = jnp.einsum("ctv,ctl->cvl", vs, ks, preferred_element_type=F32)

    state_t = state_ref[...]
    states = [None] * nch
    for c in (range(nch) if forward else range(nch - 1, -1, -1)):
        states[c] = state_t.astype(BF16)
        state_t = state_t * decay[c] + kv_t[c]
    state_ref[...] = state_t
    inter = jnp.einsum("cql,cvl->cqv", qs, jnp.stack(states), preferred_element_type=F32)
    o = jnp.concatenate([intra[:, :C, :GLA_DV] + inter[:, :C], intra[:, C:, GLA_DV:] + inter[:, C:]],
                        axis=2).reshape(TS, 2 * GLA_DV)
    if forward:
        tot = o + ob_ref[0].astype(F32)
        r = r_ref[0].astype(F32)
        gate = r / (1.0 + jnp.exp(-r))
        for hh in range(2):
            hs = slice(hh * GLA_DV, (hh + 1) * GLA_DV)
            o_ref[0, :, hs] = (_rms(tot[:, hs], g_ref[...]) * gate[:, hs]).astype(o_ref.dtype)
    else:
        o_ref[0] = o.astype(o_ref.dtype)


def _gla_pass(gq, gk, la, gv, ts, forward, ob=None, gr=None, g_gla=None):
    B, S, _ = gq.shape
    nt = S // ts
    tile = (lambda i: i) if forward else (lambda i: nt - 1 - i)
    la_off = 0 if forward else GLA_KEY_WIDTH // LANES
    key = pl.BlockSpec((1, ts, LANES), lambda b, h, i: (b, tile(i), h))
    gate = pl.BlockSpec((1, ts, LANES), lambda b, h, i: (b, tile(i), la_off + h))
    val = pl.BlockSpec((1, ts, 2 * GLA_DV), lambda b, h, i: (b, tile(i), h))
    in_specs = [key, key, gate, val]
    args = [gq, gk, la, gv]
    if forward:
        in_specs += [val, val, pl.BlockSpec(g_gla.shape, lambda b, h, i: (0, 0))]
        args += [ob, gr, g_gla]
    return pl.pallas_call(
        functools.partial(_gla_kernel, TS=ts, forward=forward),
        grid=(B, N_HEADS_GLA // 2, nt),
        in_specs=in_specs,
        out_specs=val,
        out_shape=jax.ShapeDtypeStruct((B, S, GLA_WIDTH), BF16),
        scratch_shapes=[pltpu.VMEM((GLA_DV, 2 * GLA_DK), F32)],
        compiler_params=pltpu.CompilerParams(
            dimension_semantics=("parallel", "parallel", "arbitrary"), vmem_limit_bytes=VMEM_LIMIT),
        name="gla_fwd" if forward else "gla_bwd",
    )(*args)


def _tail_kernel(x_ref, od_ref, og_ref, wo_ref, gf_ref, wg_ref, wu_ref, wd_ref, gl_ref, y_ref, *, ff_chunk):
    mix = (jnp.dot(od_ref[...], wo_ref[:DIL_WIDTH, :], preferred_element_type=F32)
           + jnp.dot(og_ref[...], wo_ref[DIL_WIDTH:, :], preferred_element_type=F32))
    x1 = x_ref[...] + mix
    h = _rms(x1, gf_ref[...]).astype(BF16)
    acc = x1
    for c0 in range(0, D_FF, ff_chunk):
        gate = jnp.dot(h, wg_ref[:, c0:c0 + ff_chunk], preferred_element_type=F32)
        up = jnp.dot(h, wu_ref[:, c0:c0 + ff_chunk], preferred_element_type=F32)
        act = (gate / (1.0 + jnp.exp(-gate)) * up).astype(BF16)
        acc = acc + jnp.dot(act, wd_ref[c0:c0 + ff_chunk, :], preferred_element_type=F32)
    y_ref[...] = _rms(acc, gl_ref[...])


def _tail(x, o_dil, o_gla, w_out, g_ffn, w_gate, w_up, w_down, g_final, tm, ff_chunk):
    N, D = x.shape
    tok = lambda w: pl.BlockSpec((tm, w), lambda i: (i, 0))
    full = lambda a: pl.BlockSpec(a.shape, lambda i: (0, 0), pipeline_mode=pl.Buffered(1))
    return pl.pallas_call(
        functools.partial(_tail_kernel, ff_chunk=ff_chunk),
        grid=(N // tm,),
        in_specs=[tok(D), tok(DIL_WIDTH), tok(GLA_WIDTH), full(w_out), full(g_ffn),
                  full(w_gate), full(w_up), full(w_down), full(g_final)],
        out_specs=tok(D),
        out_shape=jax.ShapeDtypeStruct((N, D), F32),
        compiler_params=pltpu.CompilerParams(
            dimension_semantics=("parallel",), vmem_limit_bytes=VMEM_LIMIT),
        name="tail",
    )(x, o_dil, o_gla, w_out, g_ffn, w_gate, w_up, w_down, g_final)


def _trunk(x, p, tm_in=1024, ts_gla=4096, tm_tail=1024, ff_chunk=256):
    B, S, D = x.shape
    qkv_nat, qkv_d4, qkv_d16, gq, gk, gv, gr, la = _inproj(
        x, p["g_mix"], p["w_in"], p["w_gate"], p["b_gate"], tm_in)
    o_dil = _dilated_mixture(qkv_nat, qkv_d4, qkv_d16)
    ob = _gla_pass(gq, gk, la, gv, ts_gla, forward=False)
    o_gla = _gla_pass(gq, gk, la, gv, ts_gla, forward=True, ob=ob, gr=gr, g_gla=p["g_gla"])
    y = _tail(x.reshape(B * S, D), o_dil.reshape(B * S, DIL_WIDTH), o_gla.reshape(B * S, GLA_WIDTH),
              p["w_out"], p["g_ffn"], p["w_ffn_gate"], p["w_ffn_up"], p["w_ffn_down"], p["g_final"],
              tm_tail, ff_chunk)
    return y.reshape(B, S, D)


def _prepare_params(g_mix, w_in, w_gate_f, b_gate_f, w_gate_b, b_gate_b, g_gla, w_out,
                    g_ffn, w_ffn_gate, w_ffn_up, w_ffn_down, g_final):
    z = jnp.zeros_like(w_gate_f[0])
    w_gate = jnp.concatenate(
        [jnp.concatenate([w_gate_f[0], z], axis=1), jnp.concatenate([z, w_gate_b[0]], axis=1)], axis=0)
    return {
        "g_mix": g_mix[0][None, :],
        "w_in": w_in[0].astype(BF16),
        "w_gate": w_gate.astype(BF16),
        "b_gate": jnp.concatenate([b_gate_f[0], b_gate_b[0]])[None, :],
        "g_gla": g_gla[0][None, :],
        "w_out": w_out[0].astype(BF16),
        "g_ffn": g_ffn[0][None, :],
        "w_ffn_gate": w_ffn_gate[0].astype(BF16),
        "w_ffn_up": w_ffn_up[0].astype(BF16),
        "w_ffn_down": w_ffn_down[0].astype(BF16),
        "g_final": g_final[None, :],
    }


def kernel(x_prompt, x_sample, g_mix, w_in, w_gate_f, b_gate_f, w_gate_b, b_gate_b, g_gla, w_out,
           g_ffn, w_ffn_gate, w_ffn_up, w_ffn_down, g_final):
    p = _prepare_params(g_mix, w_in, w_gate_f, b_gate_f, w_gate_b, b_gate_b, g_gla, w_out,
                        g_ffn, w_ffn_gate, w_ffn_up, w_ffn_down, g_final)
    return (_trunk(x_prompt, p), _trunk(x_sample, p))
```

```python
import functools

import jax
import jax.numpy as jnp
from jax import lax
from jax.experimental import pallas as pl
from jax.experimental.pallas import tpu as pltpu

D_MODEL = 1024
HEAD_DIM = 64
N_HEADS_DIL = 8
DIL_WIDTH = N_HEADS_DIL * HEAD_DIM
N_HEADS_GLA = 4
GLA_DK = 64
GLA_DV = 128
GLA_KEY_WIDTH = N_HEADS_GLA * GLA_DK
GLA_WIDTH = N_HEADS_GLA * GLA_DV
GLA_RANK = 16
GLA_TAU = 16.0
GLA_CHUNK = 64
DILATED_PATTERNS = ((128, 1), (512, 4), (2048, 16))
ROT_DIM = HEAD_DIM // 4
ROPE_THETA = 500000.0
D_FF = 2816
EPS = 1e-6
NEG_INF = -1e30
LOG2_E = 1.4426950408889634
PROJ_WIDTH = 3 * DIL_WIDTH + 2 * GLA_KEY_WIDTH + 2 * GLA_WIDTH + 2 * GLA_RANK
QKV_WIDTH = 3 * DIL_WIDTH

LANES = 128
BAND_HALF = 64
Q_TILE = 128
K_WINDOW = Q_TILE + 2 * BAND_HALF
DIL_STEP = 4
VMEM_LIMIT = 56 * 1024 * 1024

F32 = jnp.float32
BF16 = jnp.bfloat16


def _rms(x, g):
    return x * lax.rsqrt(jnp.mean(x * x, axis=-1, keepdims=True) + EPS) * g


def _inproj_kernel(x_ref, g_ref, w_ref, wg_ref, bg_ref, cos_ref, sa_ref, sb_ref,
                   nat_ref, d4_ref, d16_ref, gq_ref, gk_ref, gv_ref, gr_ref, la_ref,
                   slab, slab4):
    TM = x_ref.shape[1]
    R4 = TM // DIL_STEP
    R16 = R4 // DIL_STEP
    h = _rms(x_ref[0], g_ref[...]).astype(BF16)

    def proj(lo, hi):
        return jnp.dot(h, w_ref[:, lo:hi], preferred_element_type=F32)

    cos_t, sin_a, sin_b = cos_ref[...], sa_ref[...], sb_ref[...]

    def emit(yb, blk):
        cs = slice(blk * LANES, (blk + 1) * LANES)
        nat_ref[0, :, cs] = yb.astype(BF16)
        slab[...] = yb
        for r in range(DIL_STEP):
            slab4[r * R4:(r + 1) * R4, :] = slab[pl.ds(r, R4, stride=DIL_STEP), :]
        for r in range(DIL_STEP):
            d4_ref[0, r, :, cs] = slab4[r * R4:(r + 1) * R4, :].astype(BF16)
            for j in range(DIL_STEP):
                d16_ref[0, r + DIL_STEP * j, :, cs] = slab4[
                    pl.ds(r * R4 + j, R16, stride=DIL_STEP), :].astype(BF16)

    def rotary(xb):
        return (xb * cos_t + pltpu.roll(xb, LANES - ROT_DIM // 2, 1) * sin_a
                + pltpu.roll(xb, ROT_DIM // 2, 1) * sin_b)

    o = PROJ_WIDTH - 2 * GLA_RANK
    g_lr = proj(o, o + 2 * GLA_RANK).astype(BF16)
    pre = jnp.dot(g_lr, wg_ref[...], preferred_element_type=F32) + bg_ref[...]
    la_ref[0] = (jnp.minimum(pre, 0.0) - jnp.log(1.0 + jnp.exp(-jnp.abs(pre)))) * (1.0 / GLA_TAU)

    nblk = DIL_WIDTH // LANES
    t = proj(0, DIL_WIDTH)
    for c in range(nblk):
        emit(rotary(t[:, c * LANES:(c + 1) * LANES]) * (HEAD_DIM ** -0.5 * LOG2_E), c)
    t = proj(DIL_WIDTH, 2 * DIL_WIDTH)
    for c in range(nblk):
        emit(rotary(t[:, c * LANES:(c + 1) * LANES]), nblk + c)
    t = proj(2 * DIL_WIDTH, 3 * DIL_WIDTH)
    for c in range(nblk):
        emit(t[:, c * LANES:(c + 1) * LANES], 2 * nblk + c)
    o = QKV_WIDTH
    gq_ref[0] = (proj(o, o + GLA_KEY_WIDTH) * (GLA_DK ** -0.5)).astype(BF16)
    o += GLA_KEY_WIDTH
    gk_ref[0] = proj(o, o + GLA_KEY_WIDTH).astype(BF16)
    o += GLA_KEY_WIDTH
    gv_ref[0] = proj(o, o + GLA_WIDTH).astype(BF16)
    o += GLA_WIDTH
    gr_ref[0] = proj(o, o + GLA_WIDTH).astype(BF16)


def _rotary_tables(S):
    half = ROT_DIM // 2
    inv_freq = jnp.power(jnp.float32(ROPE_THETA), -jnp.arange(0, ROT_DIM, 2, dtype=F32) / ROT_DIM)
    ang = jnp.arange(S, dtype=F32)[:, None] * inv_freq[None, :]
    cos, sin = jnp.cos(ang), jnp.sin(ang)
    ones = jnp.ones((S, HEAD_DIM - ROT_DIM), F32)
    zeros = jnp.zeros((S, HEAD_DIM - ROT_DIM), F32)
    zh = jnp.zeros((S, half), F32)
    cos_t = jnp.concatenate([cos, cos, ones], axis=1)
    sin_a = jnp.concatenate([-sin, zh, zeros], axis=1)
    sin_b = jnp.concatenate([zh, sin, zeros], axis=1)
    rep = LANES // HEAD_DIM
    return tuple(jnp.tile(t, (1, rep)) for t in (cos_t, sin_a, sin_b))


def _inproj(x, g_mix, w_in, w_gate, b_gate, tables, tm):
    B, S, D = x.shape
    d4, d16 = DIL_STEP, DIL_STEP * DIL_STEP
    cos_t, sin_a, sin_b = tables
    assert cos_t.shape[0] >= S
    tok = lambda w: pl.BlockSpec((1, tm, w), lambda i, b: (b, i, 0))
    res = lambda d: pl.BlockSpec((1, d, tm // d, QKV_WIDTH), lambda i, b: (b, 0, i, 0))
    full = lambda a: pl.BlockSpec(a.shape, lambda i, b: (0,) * a.ndim, pipeline_mode=pl.Buffered(1))
    tab = pl.BlockSpec((tm, LANES), lambda i, b: (i, 0))
    out_shape = (
        jax.ShapeDtypeStruct((B, S, QKV_WIDTH), BF16),
        jax.ShapeDtypeStruct((B, d4, S // d4, QKV_WIDTH), BF16),
        jax.ShapeDtypeStruct((B, d16, S // d16, QKV_WIDTH), BF16),
        jax.ShapeDtypeStruct((B, S, GLA_KEY_WIDTH), BF16),
        jax.ShapeDtypeStruct((B, S, GLA_KEY_WIDTH), BF16),
        jax.ShapeDtypeStruct((B, S, GLA_WIDTH), BF16),
        jax.ShapeDtypeStruct((B, S, GLA_WIDTH), BF16),
        jax.ShapeDtypeStruct((B, S, 2 * GLA_KEY_WIDTH), F32),
    )
    out_specs = (tok(QKV_WIDTH), res(d4), res(d16)) + tuple(tok(s.shape[-1]) for s in out_shape[3:])
    return pl.pallas_call(
        _inproj_kernel,
        grid=(S // tm, B),
        in_specs=[tok(D), full(g_mix), full(w_in), full(w_gate), full(b_gate), tab, tab, tab],
        out_specs=out_specs,
        out_shape=out_shape,
        scratch_shapes=[pltpu.VMEM((tm, LANES), F32)] * 2,
        compiler_params=pltpu.CompilerParams(
            dimension_semantics=("parallel", "parallel"), vmem_limit_bytes=VMEM_LIMIT),
        name="inproj",
    )(x, g_mix, w_in, w_gate, b_gate, cos_t, sin_a, sin_b)


def _dil_kernel(*refs, L, pos_axis, has_prev, final):
    q_ref, kl_ref, km_ref, kr_ref, vl_ref, vm_ref, vr_ref = refs[:7]
    refs = refs[7:]
    if has_prev:
        op_ref, lp_ref = refs[:2]
        refs = refs[2:]
    if final:
        o_ref, kwin, vwin, so = refs
    else:
        o_ref, l_ref, kwin, vwin, so, sl = refs
    NSEQ, TL, CW = q_ref.shape
    i = pl.program_id(pos_axis)

    kwin[:, 0:BAND_HALF] = kl_ref[...]
    kwin[:, BAND_HALF:BAND_HALF + TL] = km_ref[...]
    kwin[:, BAND_HALF + TL:] = kr_ref[...]
    vwin[:, 0:BAND_HALF] = vl_ref[...]
    vwin[:, BAND_HALF:BAND_HALF + TL] = vm_ref[...]
    vwin[:, BAND_HALF + TL:] = vr_ref[...]

    lane = lax.broadcasted_iota(jnp.int32, (1, LANES), 1)
    head0 = lane < HEAD_DIM
    m0 = head0.astype(BF16)
    m1 = (1.0 - head0.astype(F32)).astype(BF16)
    row = lax.broadcasted_iota(jnp.int32, (Q_TILE, K_WINDOW), 0)
    col = lax.broadcasted_iota(jnp.int32, (Q_TILE, K_WINDOW), 1)
    band = (col >= row) & (col <= row + 2 * BAND_HALF)
    ones_v = jnp.ones((K_WINDOW, LANES), BF16)

    for j in range(TL // Q_TILE):
        r0 = j * Q_TILE
        kpos = i * TL + (r0 - BAND_HALF) + col
        valid = band & (kpos >= 0) & (kpos < L)
        bias = jnp.where(valid, 0.0, NEG_INF)
        bias2 = jnp.concatenate([bias, bias], axis=0)
        for n in range(NSEQ):
            for hp in range(CW // LANES):
                cs = slice(hp * LANES, (hp + 1) * LANES)
                q = q_ref[n, r0:r0 + Q_TILE, cs]
                k = kwin[n, r0:r0 + K_WINDOW, cs]
                v = vwin[n, r0:r0 + K_WINDOW, cs]
                qq = jnp.concatenate([q * m0, q * m1], axis=0)
                s = lax.dot_general(qq, k, (((1,), (1,)), ((), ())),
                                    preferred_element_type=F32) + bias2
                m = jnp.max(s, axis=1, keepdims=True)
                p = jnp.exp2(s - m).astype(BF16)
                pv = jnp.dot(p, jnp.concatenate([v, ones_v], axis=1),
                             preferred_element_type=F32)
                acc = jnp.where(head0, pv[:Q_TILE, :LANES], pv[Q_TILE:, :LANES])
                ll = jnp.where(head0, pv[:Q_TILE, LANES:], pv[Q_TILE:, LANES:])
                mm = jnp.where(head0, m[:Q_TILE], m[Q_TILE:])
                if has_prev:
                    op = op_ref[n, r0:r0 + Q_TILE, cs].astype(F32)
                    lp = lp_ref[n, r0:r0 + Q_TILE, cs]
                    mn = jnp.maximum(lp, mm)
                    w1 = jnp.exp2(lp - mn)
                    w2 = jnp.exp2(mm - mn)
                    den = w1 + ll * w2
                    o = (op * w1 + acc * w2) / den
                    lse = mn + jnp.log2(den)
                else:
                    o = acc / ll
                    lse = mm + jnp.log2(ll)
                if final:
                    so[hp, pl.ds(n + NSEQ * r0, Q_TILE, stride=NSEQ), :] = o
                else:
                    so[hp, r0:r0 + Q_TILE, :] = o
                    sl[hp, r0:r0 + Q_TILE, :] = lse

    for hp in range(CW // LANES):
        cs = slice(hp * LANES, (hp + 1) * LANES)
        if final:
            o_ref[:, cs] = so[hp].astype(o_ref.dtype)
        else:
            TQ = TL // DIL_STEP
            for r in range(DIL_STEP):
                o_ref[r, :, cs] = so[hp, pl.ds(r, TQ, stride=DIL_STEP), :].astype(o_ref.dtype)
                l_ref[r, :, cs] = sl[hp, pl.ds(r, TQ, stride=DIL_STEP), :]


def _dil_call(qkv, prev, L, grid, blk, in_idx, halo_l, halo_r, state_spec, out_specs, out_shape,
              pos_axis, final, name):
    NSEQ, TL, CW = blk
    main = lambda t: pl.BlockSpec((None, NSEQ, TL, CW), functools.partial(in_idx, t=t))
    left = lambda t: pl.BlockSpec((None, NSEQ, BAND_HALF, CW), functools.partial(halo_l, t=t))
    right = lambda t: pl.BlockSpec((None, NSEQ, BAND_HALF, CW), functools.partial(halo_r, t=t))
    in_specs = [main(0), left(1), main(1), right(1), left(2), main(2), right(2)]
    args = [qkv] * 7
    if prev is not None:
        in_specs += [state_spec, state_spec]
        args += list(prev)
    nhp = CW // LANES
    scratch = [pltpu.VMEM((NSEQ, TL + 2 * BAND_HALF, CW), BF16)] * 2
    scratch += [pltpu.VMEM((nhp, NSEQ * TL, LANES), F32)] * (1 if final else 2)
    return pl.pallas_call(
        functools.partial(_dil_kernel, L=L, pos_axis=pos_axis, has_prev=prev is not None, final=final),
        grid=grid,
        in_specs=in_specs,
        out_specs=out_specs,
        out_shape=out_shape,
        scratch_shapes=scratch,
        compiler_params=pltpu.CompilerParams(
            dimension_semantics=("parallel",) * len(grid), vmem_limit_bytes=VMEM_LIMIT),
        name=name,
    )(*args)


def _dilated_mixture(qkv_nat, qkv_d4, qkv_d16):
    B, S, _ = qkv_nat.shape
    W = DIL_WIDTH
    d4, d16 = DIL_STEP, DIL_STEP * DIL_STEP
    L4, L16 = S // d4, S // d16
    MAX_ROWS = 2048

    TL = min(MAX_ROWS, S)
    nb = TL // BAND_HALF
    st4 = lambda dt: jax.ShapeDtypeStruct((B, d4, L4, W), dt)
    spec4 = pl.BlockSpec((None, d4, TL // d4, W), lambda b, i: (b, 0, i, 0))
    state = _dil_call(
        qkv_nat.reshape(B, 1, S, QKV_WIDTH), None, S, (B, S // TL), (1, TL, W),
        lambda b, i, t: (b, 0, i, t),
        lambda b, i, t: (b, 0, jnp.maximum(i * nb - 1, 0), t),
        lambda b, i, t: (b, 0, jnp.minimum((i + 1) * nb, S // BAND_HALF - 1), t),
        None, (spec4, spec4), (st4(BF16), st4(F32)), pos_axis=1, final=False, name="dilated_d1")

    TL = min(MAX_ROWS, L4)
    nb = TL // BAND_HALF
    st16 =lambda dt: jax.ShapeDtypeStruct((B, d4, d4, L16, W), dt)
    spec16 = pl.BlockSpec((None, d4, None, TL // d4, W), lambda b, r, i: (b, 0, r, i, 0))
    state = _dil_call(
        qkv_d4, state, L4, (B, d4, L4 // TL), (1, TL, W),
        lambda b, r, i, t: (b, r, i, t),
        lambda b, r, i, t: (b, r, jnp.maximum(i * nb - 1, 0), t),
        lambda b, r, i, t: (b, r, jnp.minimum((i + 1) * nb, L4 // BAND_HALF - 1), t),
        pl.BlockSpec((None, 1, TL, W), lambda b, r, i: (b, r, i, 0)),
        (spec16, spec16), (st16(BF16), st16(F32)), pos_axis=2, final=False, name="dilated_d4")
    state = tuple(s.reshape(B, d16, L16, W) for s in state)

    CW = LANES
    TL16 = 2 * Q_TILE
    nbq = TL16 // BAND_HALF
    span = d16 * TL16
    return _dil_call(
        qkv_d16, state, L16, (B, S // span, W // CW), (d16, TL16, CW),
        lambda b, i, c, t: (b, 0, i, t * (W // CW) + c),
        lambda b, i, c, t: (b, 0, jnp.maximum(i * nbq - 1, 0), t * (W // CW) + c),
        lambda b, i, c, t: (b, 0, jnp.minimum((i + 1) * nbq, L16 // BAND_HALF - 1), t * (W // CW) + c),
        pl.BlockSpec((None, d16, TL16, CW), lambda b, i, c: (b, 0, i, c)),
        pl.BlockSpec((None, span, CW), lambda b, i, c: (b, i, c)),
        jax.ShapeDtypeStruct((B, S, W), BF16), pos_axis=1, final=True, name="dilated_d16")


def _gla_kernel(*refs, TS, forward):
    if forward:
        q_ref, k_ref, la_ref, v_ref, ob_ref, r_ref, g_ref, o_ref, state_ref = refs
    else:
        q_ref, k_ref, la_ref, v_ref, o_ref, state_ref = refs
    C = GLA_CHUNK
    nch = TS // C

    @pl.when(pl.program_id(2) == 0)
    def _():
        state_ref[...] = jnp.zeros_like(state_ref)

    lane = lax.broadcasted_iota(jnp.int32, (1, LANES), 1)
    head0 = (lane < GLA_DK)[None]
    tr = lax.broadcasted_iota(jnp.int32, (C, C), 0)
    tc = lax.broadcasted_iota(jnp.int32, (C, C), 1)
    keep = (tc <= tr) if forward else (tc > tr)
    keep2 = jnp.concatenate([keep, keep], axis=0)[None]

    b = la_ref[0] * LOG2_E
    rowc = lax.broadcasted_iota(jnp.int32, (TS, LANES), 0) % C
    step = 1
    while step < C:
        if forward:
            b = b + jnp.where(rowc >= step, pltpu.roll(b, step, 0), 0.0)
        else:
            b = b + jnp.where(rowc < C - step, pltpu.roll(b, TS - step, 0), 0.0)
        step *= 2
    b = b.reshape(nch, C, LANES)
    total = b[:, C - 1:C, :] if forward else b[:, 0:1, :]

    q = q_ref[0].reshape(nch, C, LANES)
    k = k_ref[0].reshape(nch, C, LANES)
    v = v_ref[0].reshape(nch, C, 2 * GLA_DV)
    decay = jnp.exp2(total)
    q_in = (q * jnp.exp2(b)).astype(BF16)
    k_inf = k * jnp.exp2(-b)
    k_in = k_inf.astype(BF16)
    k_dec = (k_inf * decay).astype(BF16)
    zero = jnp.zeros_like(q_in)
    qs = jnp.concatenate([jnp.where(head0, q_in, zero), jnp.where(head0, zero, q_in)], axis=1)
    att = jnp.einsum("cqd,ckd->cqk", qs, k_in, preferred_element_type=F32)
    att = jnp.where(keep2, att, 0.0).astype(BF16)
    intra = jnp.einsum("cqk,ckv->cqv", att, v, preferred_element_type=F32)
    ks = jnp.concatenate([jnp.where(head0, k_dec, zero), jnp.where(head0, zero, k_dec)], axis=1)
    vs = jnp.concatenate([v[:, :, :GLA_DV], v[:, :, GLA_DV:]], axis=1)
    kv_t = jnp.einsum("ctv,ctl->cvl", vs, ks, preferred_element_type=F32)

    state_t = state_ref[...]
    states = [None] * nch
    for c in (range(nch) if forward else range(nch - 1, -1, -1)):
        states[c] = state_t.astype(BF16)
        state_t = state_t * decay[c] + kv_t[c]
    state_ref[...] = state_t
    inter = jnp.einsum("cql,cvl->cqv", qs, jnp.stack(states), preferred_element_type=F32)
    o = jnp.concatenate([intra[:, :C, :GLA_DV] + inter[:, :C], intra[:, C:, GLA_DV:] + inter[:, C:]],
                        axis=2).reshape(TS, 2 * GLA_DV)
    if forward:
        tot = o + ob_ref[0].astype(F32)
        r = r_ref[0].astype(F32)
        gate = r / (1.0 + jnp.exp(-r))
        for hh in range(2):
            hs = slice(hh * GLA_DV, (hh + 1) * GLA_DV)
            o_ref[0, :, hs] = (_rms(tot[:, hs], g_ref[...]) * gate[:, hs]).astype(o_ref.dtype)
    else:
        o_ref[0] = o.astype(o_ref.dtype)


def _gla_pass(gq, gk, la, gv, ts, forward, ob=None, gr=None, g_gla=None):
    B, S, _ = gq.shape
    nt = S // ts
    tile = (lambda i: i) if forward else (lambda i: nt - 1 - i)
    la_off = 0 if forward else GLA_KEY_WIDTH // LANES
    key = pl.BlockSpec((1, ts, LANES), lambda b, h, i: (b, tile(i), h))
    gate = pl.BlockSpec((1, ts, LANES), lambda b, h, i: (b, tile(i), la_off + h))
    val = pl.BlockSpec((1, ts, 2 * GLA_DV), lambda b, h, i: (b, tile(i), h))
    in_specs = [key, key, gate, val]
    args = [gq, gk, la, gv]
    if forward:
        in_specs += [val, val, pl.BlockSpec(g_gla.shape, lambda b, h, i: (0, 0))]
        args += [ob, gr, g_gla]
    return pl.pallas_call(
        functools.partial(_gla_kernel, TS=ts, forward=forward),
        grid=(B, N_HEADS_GLA // 2, nt),
        in_specs=in_specs,
        out_specs=val,
        out_shape=jax.ShapeDtypeStruct((B, S, GLA_WIDTH), BF16),
        scratch_shapes=[pltpu.VMEM((GLA_DV, 2 * GLA_DK), F32)],
        compiler_params=pltpu.CompilerParams(
            dimension_semantics=("parallel", "parallel", "arbitrary"), vmem_limit_bytes=VMEM_LIMIT),
        name="gla_fwd" if forward else "gla_bwd",
    )(*args)


def _tail_kernel(x_ref, od_ref, og_ref, wo_ref, gf_ref, wg_ref, wu_ref, wd_ref, gl_ref, y_ref, *, ff_chunk):
    mix = (jnp.dot(od_ref[...], wo_ref[:DIL_WIDTH, :], preferred_element_type=F32)
           + jnp.dot(og_ref[...], wo_ref[DIL_WIDTH:, :], preferred_element_type=F32))
    x1 = x_ref[...] + mix
    h = _rms(x1, gf_ref[...]).astype(BF16)
    acc = x1
    for c0 in range(0, D_FF, ff_chunk):
        gate = jnp.dot(h, wg_ref[:, c0:c0 + ff_chunk], preferred_element_type=F32)
        up = jnp.dot(h, wu_ref[:, c0:c0 + ff_chunk], preferred_element_type=F32)
        act = (gate / (1.0 + jnp.exp(-gate)) * up).astype(BF16)
        acc = acc + jnp.dot(act, wd_ref[c0:c0 + ff_chunk, :], preferred_element_type=F32)
    y_ref[...] = _rms(acc, gl_ref[...])


def _tail(x, o_dil, o_gla, w_out, g_ffn, w_gate, w_up, w_down, g_final, tm, ff_chunk):
    N, D = x.shape
    tok = lambda w: pl.BlockSpec((tm, w), lambda i: (i, 0))
    full = lambda a: pl.BlockSpec(a.shape, lambda i: (0, 0), pipeline_mode=pl.Buffered(1))
    return pl.pallas_call(
        functools.partial(_tail_kernel, ff_chunk=ff_chunk),
        grid=(N // tm,),
        in_specs=[tok(D), tok(DIL_WIDTH), tok(GLA_WIDTH), full(w_out), full(g_ffn),
                  full(w_gate), full(w_up), full(w_down), full(g_final)],
        out_specs=tok(D),
        out_shape=jax.ShapeDtypeStruct((N, D), F32),
        compiler_params=pltpu.CompilerParams(
            dimension_semantics=("parallel",), vmem_limit_bytes=VMEM_LIMIT),
        name="tail",
    )(x, o_dil, o_gla, w_out, g_ffn, w_gate, w_up, w_down, g_final)


def _trunk(x, p, tables, tm_in=1024, ts_gla=4096, tm_tail=1024, ff_chunk=256):
    B, S, D = x.shape
    qkv_nat, qkv_d4, qkv_d16, gq, gk, gv, gr, la = _inproj(
        x, p["g_mix"], p["w_in"], p["w_gate"], p["b_gate"], tables, tm_in)
    o_dil = _dilated_mixture(qkv_nat, qkv_d4, qkv_d16)
    ob = _gla_pass(gq, gk, la, gv, ts_gla, forward=False)
    o_gla = _gla_pass(gq, gk, la, gv, ts_gla, forward=True, ob=ob, gr=gr, g_gla=p["g_gla"])
    y = _tail(x.reshape(B * S, D), o_dil.reshape(B * S, DIL_WIDTH), o_gla.reshape(B * S, GLA_WIDTH),
              p["w_out"], p["g_ffn"], p["w_ffn_gate"], p["w_ffn_up"], p["w_ffn_down"], p["g_final"],
              tm_tail, ff_chunk)
    return y.reshape(B, S, D)


def _prepare_params(g_mix, w_in, w_gate_f, b_gate_f, w_gate_b, b_gate_b, g_gla, w_out,
                    g_ffn, w_ffn_gate, w_ffn_up, w_ffn_down, g_final):
    assert w_in.shape[0] == 1, "single-layer stack expected"
    z = jnp.zeros_like(w_gate_f[0])
    w_gate = jnp.concatenate(
        [jnp.concatenate([w_gate_f[0], z], axis=1), jnp.concatenate([z, w_gate_b[0]], axis=1)], axis=0)
    return {
        "g_mix": g_mix[0][None, :],
        "w_in": w_in[0].astype(BF16),
        "w_gate": w_gate.astype(BF16),
        "b_gate": jnp.concatenate([b_gate_f[0], b_gate_b[0]])[None, :],
        "g_gla": g_gla[0][None, :],
        "w_out": w_out[0].astype(BF16),
        "g_ffn": g_ffn[0][None, :],
        "w_ffn_gate": w_ffn_gate[0].astype(BF16),
        "w_ffn_up": w_ffn_up[0].astype(BF16),
        "w_ffn_down": w_ffn_down[0].astype(BF16),
        "g_final": g_final[None, :],
    }


def kernel(x_prompt, x_sample, g_mix, w_in, w_gate_f, b_gate_f, w_gate_b, b_gate_b, g_gla, w_out,
           g_ffn, w_ffn_gate, w_ffn_up, w_ffn_down, g_final):
    p = _prepare_params(g_mix, w_in, w_gate_f, b_gate_f, w_gate_b, b_gate_b, g_gla, w_out,
                        g_ffn, w_ffn_gate, w_ffn_up, w_ffn_down, g_final)
    tables = _rotary_tables(max(x_prompt.shape[1], x_sample.shape[1]))
    return (_trunk(x_prompt, p, tables), _trunk(x_sample, p, tables))
```

```python
import functools

import jax
import jax.numpy as jnp
from jax import lax
from jax.experimental import pallas as pl
from jax.experimental.pallas import tpu as pltpu

D_MODEL = 1024
HEAD_DIM = 64
N_HEADS_DIL = 8
DIL_WIDTH = N_HEADS_DIL * HEAD_DIM
N_HEADS_GLA = 4
GLA_DK = 64
GLA_DV = 128
GLA_KEY_WIDTH = N_HEADS_GLA * GLA_DK
GLA_WIDTH = N_HEADS_GLA * GLA_DV
GLA_RANK = 16
GLA_TAU = 16.0
GLA_CHUNK = 64
DILATED_PATTERNS = ((128, 1), (512, 4), (2048, 16))
ROT_DIM = HEAD_DIM // 4
ROPE_THETA = 500000.0
D_FF = 2816
EPS = 1e-6
NEG_INF = -1e30
LOG2_E = 1.4426950408889634
PROJ_WIDTH = 3 * DIL_WIDTH + 2 * GLA_KEY_WIDTH + 2 * GLA_WIDTH + 2 * GLA_RANK
QKV_WIDTH = 3 * DIL_WIDTH

LANES = 128
BAND_HALF = 64
Q_TILE = 128
K_WINDOW = Q_TILE + 2 * BAND_HALF
DIL_STEP = 4
VMEM_LIMIT = 56 * 1024 * 1024

F32 = jnp.float32
BF16 = jnp.bfloat16


def _rms(x, g):
    return x * lax.rsqrt(jnp.mean(x * x, axis=-1, keepdims=True) + EPS) * g


def _inproj_kernel(x_ref, g_ref, w_ref, wg_ref, bg_ref, cos_ref, sa_ref, sb_ref,
                   nat_ref, d4_ref, d16_ref, gq_ref, gk_ref, gv_ref, gr_ref, la_ref,
                   slab, slab4):
    TM = x_ref.shape[1]
    R4 = TM // DIL_STEP
    R16 = R4 // DIL_STEP
    h = _rms(x_ref[0], g_ref[...]).astype(BF16)

    def proj(lo, hi):
        return jnp.dot(h, w_ref[:, lo:hi], preferred_element_type=F32)

    cos_t, sin_a, sin_b = cos_ref[...], sa_ref[...], sb_ref[...]

    def emit(yb, blk):
        cs = slice(blk * LANES, (blk + 1) * LANES)
        nat_ref[0, :, cs] = yb.astype(BF16)
        slab[...] = yb
        for r in range(DIL_STEP):
            slab4[r * R4:(r + 1) * R4, :] = slab[pl.ds(r, R4, stride=DIL_STEP), :]
        for r in range(DIL_STEP):
            d4_ref[0, r, :, cs] = slab4[r * R4:(r + 1) * R4, :].astype(BF16)
            for j in range(DIL_STEP):
                d16_ref[0, r + DIL_STEP * j, :, cs] = slab4[
                    pl.ds(r * R4 + j, R16, stride=DIL_STEP), :].astype(BF16)

    def rotary(xb):
        return (xb * cos_t + pltpu.roll(xb, LANES - ROT_DIM // 2, 1) * sin_a
                + pltpu.roll(xb, ROT_DIM // 2, 1) * sin_b)

    o = PROJ_WIDTH - 2 * GLA_RANK
    g_lr = proj(o, o + 2 * GLA_RANK).astype(BF16)
    pre = jnp.dot(g_lr, wg_ref[...], preferred_element_type=F32) + bg_ref[...]
    la_ref[0] = (jnp.minimum(pre, 0.0) - jnp.log(1.0 + jnp.exp(-jnp.abs(pre)))) * (1.0 / GLA_TAU)

    nblk = DIL_WIDTH // LANES
    t = proj(0, DIL_WIDTH)
    for c in range(nblk):
        emit(rotary(t[:, c * LANES:(c + 1) * LANES]) * (HEAD_DIM ** -0.5 * LOG2_E), c)
    t = proj(DIL_WIDTH, 2 * DIL_WIDTH)
    for c in range(nblk):
        emit(rotary(t[:, c * LANES:(c + 1) * LANES]), nblk + c)
    t = proj(2 * DIL_WIDTH, 3 * DIL_WIDTH)
    for c in range(nblk):
        emit(t[:, c * LANES:(c + 1) * LANES], 2 * nblk + c)
    o = QKV_WIDTH
    gq_ref[0] = (proj(o, o + GLA_KEY_WIDTH) * (GLA_DK ** -0.5)).astype(BF16)
    o += GLA_KEY_WIDTH
    gk_ref[0] = proj(o, o + GLA_KEY_WIDTH).astype(BF16)
    o += GLA_KEY_WIDTH
    gv_ref[0] = proj(o, o + GLA_WIDTH).astype(BF16)
    o += GLA_WIDTH
    gr_ref[0] = proj(o, o + GLA_WIDTH).astype(BF16)


def _rotary_tables(S):
    half = ROT_DIM // 2
    inv_freq = jnp.power(jnp.float32(ROPE_THETA), -jnp.arange(0, ROT_DIM, 2, dtype=F32) / ROT_DIM)
    ang = jnp.arange(S, dtype=F32)[:, None] * inv_freq[None, :]
    cos, sin = jnp.cos(ang), jnp.sin(ang)
    ones = jnp.ones((S, HEAD_DIM - ROT_DIM), F32)
    zeros = jnp.zeros((S, HEAD_DIM - ROT_DIM), F32)
    zh = jnp.zeros((S, half), F32)
    cos_t = jnp.concatenate([cos, cos, ones], axis=1)
    sin_a = jnp.concatenate([-sin, zh, zeros], axis=1)
    sin_b = jnp.concatenate([zh, sin, zeros], axis=1)
    rep = LANES // HEAD_DIM
    return tuple(jnp.tile(t, (1, rep)) for t in (cos_t, sin_a, sin_b))


def _inproj(x, g_mix, w_in, w_gate, b_gate, tables, tm):
    B, S, D = x.shape
    d4, d16 = DIL_STEP, DIL_STEP * DIL_STEP
    cos_t, sin_a, sin_b = tables
    assert cos_t.shape[0] >= S
    tok = lambda w: pl.BlockSpec((1, tm, w), lambda i, b: (b, i, 0))
    res = lambda d: pl.BlockSpec((1, d, tm // d, QKV_WIDTH), lambda i, b: (b, 0, i, 0))
    full = lambda a: pl.BlockSpec(a.shape, lambda i, b: (0,) * a.ndim, pipeline_mode=pl.Buffered(1))
    tab = pl.BlockSpec((tm, LANES), lambda i, b: (i, 0))
    out_shape = (
        jax.ShapeDtypeStruct((B, S, QKV_WIDTH), BF16),
        jax.ShapeDtypeStruct((B, d4, S // d4, QKV_WIDTH), BF16),
        jax.ShapeDtypeStruct((B, d16, S // d16, QKV_WIDTH), BF16),
        jax.ShapeDtypeStruct((B, S, GLA_KEY_WIDTH), BF16),
        jax.ShapeDtypeStruct((B, S, GLA_KEY_WIDTH), BF16),
        jax.ShapeDtypeStruct((B, S, GLA_WIDTH), BF16),
        jax.ShapeDtypeStruct((B, S, GLA_WIDTH), BF16),
        jax.ShapeDtypeStruct((B, S, 2 * GLA_KEY_WIDTH), F32),
    )
    out_specs = (tok(QKV_WIDTH), res(d4), res(d16)) + tuple(tok(s.shape[-1]) for s in out_shape[3:])
    return pl.pallas_call(
        _inproj_kernel,
        grid=(S // tm, B),
        in_specs=[tok(D), full(g_mix), full(w_in), full(w_gate), full(b_gate), tab, tab, tab],
        out_specs=out_specs,
        out_shape=out_shape,
        scratch_shapes=[pltpu.VMEM((tm, LANES), F32)] * 2,
        compiler_params=pltpu.CompilerParams(
            dimension_semantics=("parallel", "parallel"), vmem_limit_bytes=VMEM_LIMIT),
        name="inproj",
    )(x, g_mix, w_in, w_gate, b_gate, cos_t, sin_a, sin_b)


def _dil_kernel(*refs, L, pos_axis, has_prev, final):
    q_ref, kl_ref, km_ref, kr_ref, vl_ref, vm_ref, vr_ref = refs[:7]
    refs = refs[7:]
    if has_prev:
        op_ref, lp_ref = refs[:2]
        refs = refs[2:]
    if final:
        o_ref, kwin, vwin, so = refs
    else:
        o_ref, l_ref, kwin, vwin, so, sl = refs
    NSEQ, TL, CW = q_ref.shape
    i = pl.program_id(pos_axis)

    kwin[:, 0:BAND_HALF] = kl_ref[...]
    kwin[:, BAND_HALF:BAND_HALF + TL] = km_ref[...]
    kwin[:, BAND_HALF + TL:] = kr_ref[...]
    vwin[:, 0:BAND_HALF] = vl_ref[...]
    vwin[:, BAND_HALF:BAND_HALF + TL] = vm_ref[...]
    vwin[:, BAND_HALF + TL:] = vr_ref[...]

    lane = lax.broadcasted_iota(jnp.int32, (1, LANES), 1)
    head0 = lane < HEAD_DIM
    m0 = head0.astype(BF16)
    m1 = (1.0 - head0.astype(F32)).astype(BF16)
    row = lax.broadcasted_iota(jnp.int32, (Q_TILE, K_WINDOW), 0)
    col = lax.broadcasted_iota(jnp.int32, (Q_TILE, K_WINDOW), 1)
    band = (col >= row) & (col <= row + 2 * BAND_HALF)
    ones_v = jnp.ones((K_WINDOW, LANES), BF16)

    for j in range(TL // Q_TILE):
        r0 = j * Q_TILE
        kpos = i * TL + (r0 - BAND_HALF) + col
        valid = band & (kpos >= 0) & (kpos < L)
        bias = jnp.where(valid, 0.0, NEG_INF)
        bias2 = jnp.concatenate([bias, bias], axis=0)
        for n in range(NSEQ):
            for hp in range(CW // LANES):
                cs = slice(hp * LANES, (hp + 1) * LANES)
                q = q_ref[n, r0:r0 + Q_TILE, cs]
                k = kwin[n, r0:r0 + K_WINDOW, cs]
                v = vwin[n, r0:r0 + K_WINDOW, cs]
                qq = jnp.concatenate([q * m0, q * m1], axis=0)
                s = lax.dot_general(qq, k, (((1,), (1,)), ((), ())),
                                    preferred_element_type=F32) + bias2
                m = jnp.max(s, axis=1, keepdims=True)
                p = jnp.exp2(s - m).astype(BF16)
                pv = jnp.dot(p, jnp.concatenate([v, ones_v], axis=1),
                             preferred_element_type=F32)
                acc = jnp.where(head0, pv[:Q_TILE, :LANES], pv[Q_TILE:, :LANES])
                ll = jnp.where(head0, pv[:Q_TILE, LANES:], pv[Q_TILE:, LANES:])
                mm = jnp.where(head0, m[:Q_TILE], m[Q_TILE:])
                if has_prev:
                    op = op_ref[n, r0:r0 + Q_TILE, cs].astype(F32)
                    lp = lp_ref[n, r0:r0 + Q_TILE, cs]
                    mn = jnp.maximum(lp, mm)
                    w1 = jnp.exp2(lp - mn)
                    w2 = jnp.exp2(mm - mn)
                    den = w1 + ll * w2
                    o = (op * w1 + acc * w2) / den
                    lse = mn + jnp.log2(den)
                else:
                    o = acc / ll
                    lse = mm + jnp.log2(ll)
                if final:
                    so[hp, pl.ds(n + NSEQ * r0, Q_TILE, stride=NSEQ), :] = o
                else:
                    so[hp, r0:r0 + Q_TILE, :] = o
                    sl[hp, r0:r0 + Q_TILE, :] = lse

    for hp in range(CW // LANES):
        cs = slice(hp * LANES, (hp + 1) * LANES)
        if final:
            o_ref[:, cs] = so[hp].astype(o_ref.dtype)
        else:
            TQ = TL // DIL_STEP
            for r in range(DIL_STEP):
                o_ref[r, :, cs] = so[hp, pl.ds(r, TQ, stride=DIL_STEP), :].astype(o_ref.dtype)
                l_ref[r, :, cs] = sl[hp, pl.ds(r, TQ, stride=DIL_STEP), :]


def _dil_call(qkv, prev, L, grid, blk, in_idx, halo_l, halo_r, state_spec, out_specs, out_shape,
              pos_axis, final, name):
    NSEQ, TL, CW = blk
    main = lambda t: pl.BlockSpec((None, NSEQ, TL, CW), functools.partial(in_idx, t=t))
    left = lambda t: pl.BlockSpec((None, NSEQ, BAND_HALF, CW), functools.partial(halo_l, t=t))
    right = lambda t: pl.BlockSpec((None, NSEQ, BAND_HALF, CW), functools.partial(halo_r, t=t))
    in_specs = [main(0), left(1), main(1), right(1), left(2), main(2), right(2)]
    args = [qkv] * 7
    if prev is not None:
        in_specs += [state_spec, state_spec]
        args += list(prev)
    nhp = CW // LANES
    scratch = [pltpu.VMEM((NSEQ, TL + 2 * BAND_HALF, CW), BF16)] * 2
    scratch += [pltpu.VMEM((nhp, NSEQ * TL, LANES), F32)] * (1 if final else 2)
    return pl.pallas_call(
        functools.partial(_dil_kernel, L=L, pos_axis=pos_axis, has_prev=prev is not None, final=final),
        grid=grid,
        in_specs=in_specs,
        out_specs=out_specs,
        out_shape=out_shape,
        scratch_shapes=scratch,
        compiler_params=pltpu.CompilerParams(
            dimension_semantics=("parallel",) * len(grid), vmem_limit_bytes=VMEM_LIMIT),
        name=name,
    )(*args)


def _dilated_mixture(qkv_nat, qkv_d4, qkv_d16):
    B, S, _ = qkv_nat.shape
    W = DIL_WIDTH
    d4, d16 = DIL_STEP, DIL_STEP * DIL_STEP
    L4, L16 = S // d4, S // d16
    MAX_ROWS = 2048

    TL = min(MAX_ROWS, S)
    nb = TL // BAND_HALF
    st4 = lambda dt: jax.ShapeDtypeStruct((B, d4, L4, W), dt)
    spec4 = pl.BlockSpec((None, d4, TL // d4, W), lambda b, i: (b, 0, i, 0))
    state = _dil_call(
        qkv_nat.reshape(B, 1, S, QKV_WIDTH), None, S, (B, S // TL), (1, TL, W),
        lambda b, i, t: (b, 0, i, t),
        lambda b, i, t: (b, 0, jnp.maximum(i * nb - 1, 0), t),
        lambda b, i, t: (b, 0, jnp.minimum((i + 1) * nb, S // BAND_HALF - 1), t),
        None, (spec4, spec4), (st4(BF16), st4(F32)), pos_axis=1, final=False, name="dilated_d1")

    TL = min(MAX_ROWS, L4)
    nb = TL // BAND_HALF
    st16 =lambda dt: jax.ShapeDtypeStruct((B, d4, d4, L16, W), dt)
    spec16 = pl.BlockSpec((None, d4, None, TL // d4, W), lambda b, r, i: (b, 0, r, i, 0))
    state = _dil_call(
        qkv_d4, state, L4, (B, d4, L4 // TL), (1, TL, W),
        lambda b, r, i, t: (b, r, i, t),
        lambda b, r, i, t: (b, r, jnp.maximum(i * nb - 1, 0), t),
        lambda b, r, i, t: (b, r, jnp.minimum((i + 1) * nb, L4 // BAND_HALF - 1), t),
        pl.BlockSpec((None, 1, TL, W), lambda b, r, i: (b, r, i, 0)),
        (spec16, spec16), (st16(BF16), st16(F32)), pos_axis=2, final=False, name="dilated_d4")
    state = tuple(s.reshape(B, d16, L16, W) for s in state)

    CW = 2 * LANES
    TL16 = 2 * Q_TILE
    nbq = TL16 // BAND_HALF
    span = d16 * TL16
    return _dil_call(
        qkv_d16, state, L16, (B, S // span, W // CW), (d16, TL16, CW),
        lambda b, i, c, t: (b, 0, i, t * (W // CW) + c),
        lambda b, i, c, t: (b, 0, jnp.maximum(i * nbq - 1, 0), t * (W // CW) + c),
        lambda b, i, c, t: (b, 0, jnp.minimum((i + 1) * nbq, L16 // BAND_HALF - 1), t * (W // CW) + c),
        pl.BlockSpec((None, d16, TL16, CW), lambda b, i, c: (b, 0, i, c)),
        pl.BlockSpec((None, span, CW), lambda b, i, c: (b, i, c)),
        jax.ShapeDtypeStruct((B, S, W), BF16), pos_axis=1, final=True, name="dilated_d16")


def _gla_kernel(*refs, TS, forward):
    if forward:
        q_ref, k_ref, la_ref, v_ref, ob_ref, r_ref, g_ref, o_ref, state_ref = refs
    else:
        q_ref, k_ref, la_ref, v_ref, o_ref, state_ref = refs
    C = GLA_CHUNK
    nch = TS // C

    @pl.when(pl.program_id(2) == 0)
    def _():
        state_ref[...] = jnp.zeros_like(state_ref)

    lane = lax.broadcasted_iota(jnp.int32, (1, LANES), 1)
    head0 = (lane < GLA_DK)[None]
    tr = lax.broadcasted_iota(jnp.int32, (C, C), 0)
    tc = lax.broadcasted_iota(jnp.int32, (C, C), 1)
    keep = (tc <= tr) if forward else (tc > tr)
    keep2 = jnp.concatenate([keep, keep], axis=0)[None]

    b = la_ref[0] * LOG2_E
    rowc = lax.broadcasted_iota(jnp.int32, (TS, LANES), 0) % C
    step = 1
    while step < C:
        if forward:
            b = b + jnp.where(rowc >= step, pltpu.roll(b, step, 0), 0.0)
        else:
            b = b + jnp.where(rowc < C - step, pltpu.roll(b, TS - step, 0), 0.0)
        step *= 2
    b = b.reshape(nch, C, LANES)
    total = b[:, C - 1:C, :] if forward else b[:, 0:1, :]

    q = q_ref[0].reshape(nch, C, LANES)
    k = k_ref[0].reshape(nch, C, LANES)
    v = v_ref[0].reshape(nch, C, 2 * GLA_DV)
    decay = jnp.exp2(total)
    q_in = (q * jnp.exp2(b)).astype(BF16)
    k_inf = k * jnp.exp2(-b)
    k_in = k_inf.astype(BF16)
    k_dec = (k_inf * decay).astype(BF16)
    zero = jnp.zeros_like(q_in)
    qs = jnp.concatenate([jnp.where(head0, q_in, zero), jnp.where(head0, zero, q_in)], axis=1)
    att = jnp.einsum("cqd,ckd->cqk", qs, k_in, preferred_element_type=F32)
    att = jnp.where(keep2, att, 0.0).astype(BF16)
    intra = jnp.einsum("cqk,ckv->cqv", att, v, preferred_element_type=F32)
    ks = jnp.concatenate([jnp.where(head0, k_dec, zero), jnp.where(head0, zero, k_dec)], axis=1)
    vs = jnp.concatenate([v[:, :, :GLA_DV], v[:, :, GLA_DV:]], axis=1)
    kv_t = jnp.einsum("ctv,ctl->cvl", vs, ks, preferred_element_type=F32)

    state_t = state_ref[...]
    states = [None] * nch
    for c in (range(nch) if forward else range(nch - 1, -1, -1)):
        states[c] = state_t.astype(BF16)
        state_t = state_t * decay[c] + kv_t[c]
    state_ref[...] = state_t
    inter = jnp.einsum("cql,cvl->cqv", qs, jnp.stack(states), preferred_element_type=F32)
    o = jnp.concatenate([intra[:, :C, :GLA_DV] + inter[:, :C], intra[:, C:, GLA_DV:] + inter[:, C:]],
                        axis=2).reshape(TS, 2 * GLA_DV)
    if forward:
        tot = o + ob_ref[0].astype(F32)
        r = r_ref[0].astype(F32)
        gate = r / (1.0 + jnp.exp(-r))
        for hh in range(2):
            hs = slice(hh * GLA_DV, (hh + 1) * GLA_DV)
            o_ref[0, :, hs] = (_rms(tot[:, hs], g_ref[...]) * gate[:, hs]).astype(o_ref.dtype)
    else:
        o_ref[0] = o.astype(o_ref.dtype)


def _gla_pass(gq, gk, la, gv, ts, forward, ob=None, gr=None, g_gla=None):
    B, S, _ = gq.shape
    nt = S // ts
    tile = (lambda i: i) if forward else (lambda i: nt - 1 - i)
    la_off = 0 if forward else GLA_KEY_WIDTH // LANES
    key = pl.BlockSpec((1, ts, LANES), lambda b, h, i: (b, tile(i), h))
    gate = pl.BlockSpec((1, ts, LANES), lambda b, h, i: (b, tile(i), la_off + h))
    val = pl.BlockSpec((1, ts, 2 * GLA_DV), lambda b, h, i: (b, tile(i), h))
    in_specs = [key, key, gate, val]
    args = [gq, gk, la, gv]
    if forward:
        in_specs += [val, val, pl.BlockSpec(g_gla.shape, lambda b, h, i: (0, 0))]
        args += [ob, gr, g_gla]
    return pl.pallas_call(
        functools.partial(_gla_kernel, TS=ts, forward=forward),
        grid=(B, N_HEADS_GLA // 2, nt),
        in_specs=in_specs,
        out_specs=val,
        out_shape=jax.ShapeDtypeStruct((B, S, GLA_WIDTH), BF16),
        scratch_shapes=[pltpu.VMEM((GLA_DV, 2 * GLA_DK), F32)],
        compiler_params=pltpu.CompilerParams(
            dimension_semantics=("parallel", "parallel", "arbitrary"), vmem_limit_bytes=VMEM_LIMIT),
        name="gla_fwd" if forward else "gla_bwd",
    )(*args)


def _tail_kernel(x_ref, od_ref, og_ref, wo_ref, gf_ref, wg_ref, wu_ref, wd_ref, gl_ref, y_ref, *, ff_chunk):
    mix = (jnp.dot(od_ref[...], wo_ref[:DIL_WIDTH, :], preferred_element_type=F32)
           + jnp.dot(og_ref[...], wo_ref[DIL_WIDTH:, :], preferred_element_type=F32))
    x1 = x_ref[...] + mix
    h = _rms(x1, gf_ref[...]).astype(BF16)
    acc = x1
    for c0 in range(0, D_FF, ff_chunk):
        gate = jnp.dot(h, wg_ref[:, c0:c0 + ff_chunk], preferred_element_type=F32)
        up = jnp.dot(h, wu_ref[:, c0:c0 + ff_chunk], preferred_element_type=F32)
        act = (gate / (1.0 + jnp.exp(-gate)) * up).astype(BF16)
        acc = acc + jnp.dot(act, wd_ref[c0:c0 + ff_chunk, :], preferred_element_type=F32)
    y_ref[...] = _rms(acc, gl_ref[...])


def _tail(x, o_dil, o_gla, w_out, g_ffn, w_gate, w_up, w_down, g_final, tm, ff_chunk):
    N, D = x.shape
    tok = lambda w: pl.BlockSpec((tm, w), lambda i: (i, 0))
    full = lambda a: pl.BlockSpec(a.shape, lambda i: (0, 0), pipeline_mode=pl.Buffered(1))
    return pl.pallas_call(
        functools.partial(_tail_kernel, ff_chunk=ff_chunk),
        grid=(N // tm,),
        in_specs=[tok(D), tok(DIL_WIDTH), tok(GLA_WIDTH), full(w_out), full(g_ffn),
                  full(w_gate), full(w_up), full(w_down), full(g_final)],
        out_specs=tok(D),
        out_shape=jax.ShapeDtypeStruct((N, D), F32),
        compiler_params=pltpu.CompilerParams(
            dimension_semantics=("parallel",), vmem_limit_bytes=VMEM_LIMIT),
        name="tail",
    )(x, o_dil, o_gla, w_out, g_ffn, w_gate, w_up, w_down, g_final)


def _trunk(x, p, tables, tm_in=1024, ts_gla=4096, tm_tail=1024, ff_chunk=256):
    B, S, D = x.shape
    qkv_nat, qkv_d4, qkv_d16, gq, gk, gv, gr, la = _inproj(
        x, p["g_mix"], p["w_in"], p["w_gate"], p["b_gate"], tables, tm_in)
    o_dil = _dilated_mixture(qkv_nat, qkv_d4, qkv_d16)
    ob = _gla_pass(gq, gk, la, gv, ts_gla, forward=False)
    o_gla = _gla_pass(gq, gk, la, gv, ts_gla, forward=True, ob=ob, gr=gr, g_gla=p["g_gla"])
    y = _tail(x.reshape(B * S, D), o_dil.reshape(B * S, DIL_WIDTH), o_gla.reshape(B * S, GLA_WIDTH),
              p["w_out"], p["g_ffn"], p["w_ffn_gate"], p["w_ffn_up"], p["w_ffn_down"], p["g_final"],
              tm_tail, ff_chunk)
    return y.reshape(B, S, D)


def _prepare_params(g_mix, w_in, w_gate_f, b_gate_f, w_gate_b, b_gate_b, g_gla, w_out,
                    g_ffn, w_ffn_gate, w_ffn_up, w_ffn_down, g_final):
    assert w_in.shape[0] == 1, "single-layer stack expected"
    z = jnp.zeros_like(w_gate_f[0])
    w_gate = jnp.concatenate(
        [jnp.concatenate([w_gate_f[0], z], axis=1), jnp.concatenate([z, w_gate_b[0]], axis=1)], axis=0)
    return {
        "g_mix": g_mix[0][None, :],
        "w_in": w_in[0].astype(BF16),
        "w_gate": w_gate.astype(BF16),
        "b_gate": jnp.concatenate([b_gate_f[0], b_gate_b[0]])[None, :],
        "g_gla": g_gla[0][None, :],
        "w_out": w_out[0].astype(BF16),
        "g_ffn": g_ffn[0][None, :],
        "w_ffn_gate": w_ffn_gate[0].astype(BF16),
        "w_ffn_up": w_ffn_up[0].astype(BF16),
        "w_ffn_down": w_ffn_down[0].astype(BF16),
        "g_final": g_final[None, :],
    }


def kernel(x_prompt, x_sample, g_mix, w_in, w_gate_f, b_gate_f, w_gate_b, b_gate_b, g_gla, w_out,
           g_ffn, w_ffn_gate, w_ffn_up, w_ffn_down, g_final):
    p = _prepare_params(g_mix, w_in, w_gate_f, b_gate_f, w_gate_b, b_gate_b, g_gla, w_out,
                        g_ffn, w_ffn_gate, w_ffn_up, w_ffn_down, g_final)
    tables = _rotary_tables(max(x_prompt.shape[1], x_sample.shape[1]))
    return (_trunk(x_prompt, p, tables), _trunk(x_sample, p, tables))
```

```python
import functools

import jax
import jax.numpy as jnp
from jax import lax
from jax.experimental import pallas as pl
from jax.experimental.pallas import tpu as pltpu

D_MODEL = 1024
HEAD_DIM = 64
N_HEADS_DIL = 8
DIL_WIDTH = N_HEADS_DIL * HEAD_DIM
N_HEADS_GLA = 4
GLA_DK = 64
GLA_DV = 128
GLA_KEY_WIDTH = N_HEADS_GLA * GLA_DK
GLA_WIDTH = N_HEADS_GLA * GLA_DV
GLA_RANK = 16
GLA_TAU = 16.0
GLA_CHUNK = 64
DILATED_PATTERNS = ((128, 1), (512, 4), (2048, 16))
ROT_DIM = HEAD_DIM // 4
ROPE_THETA = 500000.0
D_FF = 2816
EPS = 1e-6
NEG_INF = -1e30
LOG2_E = 1.4426950408889634
PROJ_WIDTH = 3 * DIL_WIDTH + 2 * GLA_KEY_WIDTH + 2 * GLA_WIDTH + 2 * GLA_RANK
QKV_WIDTH = 3 * DIL_WIDTH

LANES = 128
BAND_HALF = 64
Q_TILE = 128
K_WINDOW = Q_TILE + 2 * BAND_HALF
DIL_STEP = 4
VMEM_LIMIT = 56 * 1024 * 1024

F32 = jnp.float32
BF16 = jnp.bfloat16


def _rms(x, g):
    return x * lax.rsqrt(jnp.mean(x * x, axis=-1, keepdims=True) + EPS) * g


def _inproj_kernel(x_ref, g_ref, w_ref, wg_ref, bg_ref, cos_ref, sa_ref, sb_ref,
                   nat_ref, d4_ref, d16_ref, gq_ref, gk_ref, gv_ref, gr_ref, la_ref,
                   slab, slab4):
    TM = x_ref.shape[1]
    R4 = TM // DIL_STEP
    R16 = R4 // DIL_STEP
    h = _rms(x_ref[0], g_ref[...]).astype(BF16)

    def proj(lo, hi):
        return jnp.dot(h, w_ref[:, lo:hi], preferred_element_type=F32)

    cos_t, sin_a, sin_b = cos_ref[...], sa_ref[...], sb_ref[...]

    def emit(yb, blk):
        cs = slice(blk * LANES, (blk + 1) * LANES)
        nat_ref[0, :, cs] = yb.astype(BF16)
        slab[...] = yb
        for r in range(DIL_STEP):
            slab4[r * R4:(r + 1) * R4, :] = slab[pl.ds(r, R4, stride=DIL_STEP), :]
        for r in range(DIL_STEP):
            d4_ref[0, r, :, cs] = slab4[r * R4:(r + 1) * R4, :].astype(BF16)
            for j in range(DIL_STEP):
                d16_ref[0, r + DIL_STEP * j, :, cs] = slab4[
                    pl.ds(r * R4 + j, R16, stride=DIL_STEP), :].astype(BF16)

    def rotary(xb):
        return (xb * cos_t + pltpu.roll(xb, LANES - ROT_DIM // 2, 1) * sin_a
                + pltpu.roll(xb, ROT_DIM // 2, 1) * sin_b)

    o = PROJ_WIDTH - 2 * GLA_RANK
    g_lr = proj(o, o + 2 * GLA_RANK).astype(BF16)
    pre = jnp.dot(g_lr, wg_ref[...], preferred_element_type=F32) + bg_ref[...]
    la_ref[0] = (jnp.minimum(pre, 0.0) - jnp.log(1.0 + jnp.exp(-jnp.abs(pre)))) * (1.0 / GLA_TAU)

    nblk = DIL_WIDTH // LANES
    t = proj(0, DIL_WIDTH)
    for c in range(nblk):
        emit(rotary(t[:, c * LANES:(c + 1) * LANES]) * (HEAD_DIM ** -0.5 * LOG2_E), c)
    t = proj(DIL_WIDTH, 2 * DIL_WIDTH)
    for c in range(nblk):
        emit(rotary(t[:, c * LANES:(c + 1) * LANES]), nblk + c)
    t = proj(2 * DIL_WIDTH, 3 * DIL_WIDTH)
    for c in range(nblk):
        emit(t[:, c * LANES:(c + 1) * LANES], 2 * nblk + c)
    o = QKV_WIDTH
    gq_ref[0] = (proj(o, o + GLA_KEY_WIDTH) * (GLA_DK ** -0.5)).astype(BF16)
    o += GLA_KEY_WIDTH
    gk_ref[0] = proj(o, o + GLA_KEY_WIDTH).astype(BF16)
    o += GLA_KEY_WIDTH
    gv_ref[0] = proj(o, o + GLA_WIDTH).astype(BF16)
    o += GLA_WIDTH
    gr_ref[0] = proj(o, o + GLA_WIDTH).astype(BF16)


def _rotary_tables(S):
    half = ROT_DIM // 2
    inv_freq = jnp.power(jnp.float32(ROPE_THETA), -jnp.arange(0, ROT_DIM, 2, dtype=F32) / ROT_DIM)
    ang = jnp.arange(S, dtype=F32)[:, None] * inv_freq[None, :]
    cos, sin = jnp.cos(ang), jnp.sin(ang)
    ones = jnp.ones((S, HEAD_DIM - ROT_DIM), F32)
    zeros = jnp.zeros((S, HEAD_DIM - ROT_DIM), F32)
    zh = jnp.zeros((S, half), F32)
    cos_t = jnp.concatenate([cos, cos, ones], axis=1)
    sin_a = jnp.concatenate([-sin, zh, zeros], axis=1)
    sin_b = jnp.concatenate([zh, sin, zeros], axis=1)
    rep = LANES // HEAD_DIM
    return tuple(jnp.tile(t, (1, rep)) for t in (cos_t, sin_a, sin_b))


def _inproj(x, g_mix, w_in, w_gate, b_gate, tables, tm):
    B, S, D = x.shape
    d4, d16 = DIL_STEP, DIL_STEP * DIL_STEP
    cos_t, sin_a, sin_b = tables
    assert cos_t.shape[0] >= S
    tok = lambda w: pl.BlockSpec((1, tm, w), lambda i, b: (b, i, 0))
    res = lambda d: pl.BlockSpec((1, d, tm // d, QKV_WIDTH), lambda i, b: (b, 0, i, 0))
    full = lambda a: pl.BlockSpec(a.shape, lambda i, b: (0,) * a.ndim, pipeline_mode=pl.Buffered(1))
    tab = pl.BlockSpec((tm, LANES), lambda i, b: (i, 0))
    out_shape = (
        jax.ShapeDtypeStruct((B, S, QKV_WIDTH), BF16),
        jax.ShapeDtypeStruct((B, d4, S // d4, QKV_WIDTH), BF16),
        jax.ShapeDtypeStruct((B, d16, S // d16, QKV_WIDTH), BF16),
        jax.ShapeDtypeStruct((B, S, GLA_KEY_WIDTH), BF16),
        jax.ShapeDtypeStruct((B, S, GLA_KEY_WIDTH), BF16),
        jax.ShapeDtypeStruct((B, S, GLA_WIDTH), BF16),
        jax.ShapeDtypeStruct((B, S, GLA_WIDTH), BF16),
        jax.ShapeDtypeStruct((B, S, 2 * GLA_KEY_WIDTH), F32),
    )
    out_specs = (tok(QKV_WIDTH), res(d4), res(d16)) + tuple(tok(s.shape[-1]) for s in out_shape[3:])
    return pl.pallas_call(
        _inproj_kernel,
        grid=(S // tm, B),
        in_specs=[tok(D), full(g_mix), full(w_in), full(w_gate), full(b_gate), tab, tab, tab],
        out_specs=out_specs,
        out_shape=out_shape,
        scratch_shapes=[pltpu.VMEM((tm, LANES), F32)] * 2,
        compiler_params=pltpu.CompilerParams(
            dimension_semantics=("parallel", "parallel"), vmem_limit_bytes=VMEM_LIMIT),
        name="inproj",
    )(x, g_mix, w_in, w_gate, b_gate, cos_t, sin_a, sin_b)


def _dil_kernel(*refs, L, pos_axis, has_prev, final):
    q_ref, kl_ref, km_ref, kr_ref, vl_ref, vm_ref, vr_ref = refs[:7]
    refs = refs[7:]
    if has_prev:
        op_ref, lp_ref = refs[:2]
        refs = refs[2:]
    if final:
        o_ref, kwin, vwin, so = refs
    else:
        o_ref, l_ref, kwin, vwin, so, sl = refs
    NSEQ, TL, CW = q_ref.shape
    i = pl.program_id(pos_axis)

    kwin[:, 0:BAND_HALF] = kl_ref[...]
    kwin[:, BAND_HALF:BAND_HALF + TL] = km_ref[...]
    kwin[:, BAND_HALF + TL:] = kr_ref[...]
    vwin[:, 0:BAND_HALF] = vl_ref[...]
    vwin[:, BAND_HALF:BAND_HALF + TL] = vm_ref[...]
    vwin[:, BAND_HALF + TL:] = vr_ref[...]

    lane = lax.broadcasted_iota(jnp.int32, (1, LANES), 1)
    head0 = lane < HEAD_DIM
    m0 = head0.astype(BF16)
    m1 = (1.0 - head0.astype(F32)).astype(BF16)
    row = lax.broadcasted_iota(jnp.int32, (Q_TILE, K_WINDOW), 0)
    col = lax.broadcasted_iota(jnp.int32, (Q_TILE, K_WINDOW), 1)
    band = (col >= row) & (col <= row + 2 * BAND_HALF)
    ones_v = jnp.ones((K_WINDOW, LANES), BF16)

    for j in range(TL // Q_TILE):
        r0 = j * Q_TILE
        kpos = i * TL + (r0 - BAND_HALF) + col
        valid = band & (kpos >= 0) & (kpos < L)
        bias = jnp.where(valid, 0.0, NEG_INF)
        bias2 = jnp.concatenate([bias, bias], axis=0)
        for n in range(NSEQ):
            for hp in range(CW // LANES):
                cs = slice(hp * LANES, (hp + 1) * LANES)
                q = q_ref[n, r0:r0 + Q_TILE, cs]
                k = kwin[n, r0:r0 + K_WINDOW, cs]
                v = vwin[n, r0:r0 + K_WINDOW, cs]
                qq = jnp.concatenate([q * m0, q * m1], axis=0)
                s = lax.dot_general(qq, k, (((1,), (1,)), ((), ())),
                                    preferred_element_type=F32) + bias2
                m = jnp.max(s, axis=1, keepdims=True)
                p = jnp.exp2(s - m).astype(BF16)
                pv = jnp.dot(p, jnp.concatenate([v, ones_v], axis=1),
                             preferred_element_type=F32)
                acc = jnp.where(head0, pv[:Q_TILE, :LANES], pv[Q_TILE:, :LANES])
                ll = jnp.where(head0, pv[:Q_TILE, LANES:], pv[Q_TILE:, LANES:])
                mm = jnp.where(head0, m[:Q_TILE], m[Q_TILE:])
                if has_prev:
                    op = op_ref[n, r0:r0 + Q_TILE, cs].astype(F32)
                    lp = lp_ref[n, r0:r0 + Q_TILE, cs]
                    mn = jnp.maximum(lp, mm)
                    w1 = jnp.exp2(lp - mn)
                    w2 = jnp.exp2(mm - mn)
                    den = w1 + ll * w2
                    o = (op * w1 + acc * w2) / den
                    lse = mn + jnp.log2(den)
                else:
                    o = acc / ll
                    lse = mm + jnp.log2(ll)
                if final:
                    so[hp, pl.ds(n + NSEQ * r0, Q_TILE, stride=NSEQ), :] = o
                else:
                    so[hp, r0:r0 + Q_TILE, :] = o
                    sl[hp, r0:r0 + Q_TILE, :] = lse

    for hp in range(CW // LANES):
        cs = slice(hp * LANES, (hp + 1) * LANES)
        if final:
            o_ref[:, cs] = so[hp].astype(o_ref.dtype)
        else:
            TQ = TL // DIL_STEP
            for r in range(DIL_STEP):
                o_ref[r, :, cs] = so[hp, pl.ds(r, TQ, stride=DIL_STEP), :].astype(o_ref.dtype)
                l_ref[r, :, cs] = sl[hp, pl.ds(r, TQ, stride=DIL_STEP), :]


def _dil_call(qkv, prev, L, grid, blk, in_idx, halo_l, halo_r, state_spec, out_specs, out_shape,
              pos_axis, final, name):
    NSEQ, TL, CW = blk
    main = lambda t: pl.BlockSpec((None, NSEQ, TL, CW), functools.partial(in_idx, t=t))
    left = lambda t: pl.BlockSpec((None, NSEQ, BAND_HALF, CW), functools.partial(halo_l, t=t))
    right = lambda t: pl.BlockSpec((None, NSEQ, BAND_HALF, CW), functools.partial(halo_r, t=t))
    in_specs = [main(0), left(1), main(1), right(1), left(2), main(2), right(2)]
    args = [qkv] * 7
    if prev is not None:
        in_specs += [state_spec, state_spec]
        args += list(prev)
    nhp = CW // LANES
    scratch = [pltpu.VMEM((NSEQ, TL + 2 * BAND_HALF, CW), BF16)] * 2
    scratch += [pltpu.VMEM((nhp, NSEQ * TL, LANES), F32)] * (1 if final else 2)
    return pl.pallas_call(
        functools.partial(_dil_kernel, L=L, pos_axis=pos_axis, has_prev=prev is not None, final=final),
        grid=grid,
        in_specs=in_specs,
        out_specs=out_specs,
        out_shape=out_shape,
        scratch_shapes=scratch,
        compiler_params=pltpu.CompilerParams(
            dimension_semantics=("parallel",) * len(grid), vmem_limit_bytes=VMEM_LIMIT),
        name=name,
    )(*args)


def _dilated_mixture(qkv_nat, qkv_d4, qkv_d16):
    B, S, _ = qkv_nat.shape
    W = DIL_WIDTH
    d4, d16 = DIL_STEP, DIL_STEP * DIL_STEP
    L4, L16 = S // d4, S // d16
    MAX_ROWS = 2048

    TL = min(MAX_ROWS, S)
    nb = TL // BAND_HALF
    st4 = lambda dt: jax.ShapeDtypeStruct((B, d4, L4, W), dt)
    spec4 = pl.BlockSpec((None, d4, TL // d4, W), lambda b, i: (b, 0, i, 0))
    state = _dil_call(
        qkv_nat.reshape(B, 1, S, QKV_WIDTH), None, S, (B, S // TL), (1, TL, W),
        lambda b, i, t: (b, 0, i, t),
        lambda b, i, t: (b, 0, jnp.maximum(i * nb - 1, 0), t),
        lambda b, i, t: (b, 0, jnp.minimum((i + 1) * nb, S // BAND_HALF - 1), t),
        None, (spec4, spec4), (st4(BF16), st4(F32)), pos_axis=1, final=False, name="dilated_d1")

    TL = min(MAX_ROWS, L4)
    nb = TL // BAND_HALF
    st16 =lambda dt: jax.ShapeDtypeStruct((B, d4, d4, L16, W), dt)
    spec16 = pl.BlockSpec((None, d4, None, TL // d4, W), lambda b, r, i: (b, 0, r, i, 0))
    state = _dil_call(
        qkv_d4, state, L4, (B, d4, L4 // TL), (1, TL, W),
        lambda b, r, i, t: (b, r, i, t),
        lambda b, r, i, t: (b, r, jnp.maximum(i * nb - 1, 0), t),
        lambda b, r, i, t: (b, r, jnp.minimum((i + 1) * nb, L4 // BAND_HALF - 1), t),
        pl.BlockSpec((None, 1, TL, W), lambda b, r, i: (b, r, i, 0)),
        (spec16, spec16), (st16(BF16), st16(F32)), pos_axis=2, final=False, name="dilated_d4")
    state = tuple(s.reshape(B, d16, L16, W) for s in state)

    CW = 2 * LANES
    TL16 = 2 * Q_TILE
    nbq = TL16 // BAND_HALF
    span = d16 * TL16
    return _dil_call(
        qkv_d16, state, L16, (B, S // span, W // CW), (d16, TL16, CW),
        lambda b, i, c, t: (b, 0, i, t * (W // CW) + c),
        lambda b, i, c, t: (b, 0, jnp.maximum(i * nbq - 1, 0), t * (W // CW) + c),
        lambda b, i, c, t: (b, 0, jnp.minimum((i + 1) * nbq, L16 // BAND_HALF - 1), t * (W // CW) + c),
        pl.BlockSpec((None, d16, TL16, CW), lambda b, i, c: (b, 0, i, c)),
        pl.BlockSpec((None, span, CW), lambda b, i, c: (b, i, c)),
        jax.ShapeDtypeStruct((B, S, W), BF16), pos_axis=1, final=True, name="dilated_d16")


def _gla_kernel(*refs, TS, forward):
    q_ref, k_ref, la_ref, v_ref, o_ref, state_ref = refs
    C = GLA_CHUNK
    nch = TS // C

    @pl.when(pl.program_id(2) == 0)
    def _():
        state_ref[...] = jnp.zeros_like(state_ref)

    lane = lax.broadcasted_iota(jnp.int32, (1, LANES), 1)
    head0 = (lane < GLA_DK)[None]
    tr = lax.broadcasted_iota(jnp.int32, (C, C), 0)
    tc = lax.broadcasted_iota(jnp.int32, (C, C), 1)
    keep = (tc <= tr) if forward else (tc > tr)
    keep2 = jnp.concatenate([keep, keep], axis=0)[None]

    b = la_ref[0] * LOG2_E
    rowc = lax.broadcasted_iota(jnp.int32, (TS, LANES), 0) % C
    step = 1
    while step < C:
        if forward:
            b = b + jnp.where(rowc >= step, pltpu.roll(b, step, 0), 0.0)
        else:
            b = b + jnp.where(rowc < C - step, pltpu.roll(b, TS - step, 0), 0.0)
        step *= 2
    b = b.reshape(nch, C, LANES)
    total = b[:, C - 1:C, :] if forward else b[:, 0:1, :]

    q = q_ref[0].reshape(nch, C, LANES)
    k = k_ref[0].reshape(nch, C, LANES)
    v = v_ref[0].reshape(nch, C, 2 * GLA_DV)
    decay = jnp.exp2(total)
    q_in = (q * jnp.exp2(b)).astype(BF16)
    k_inf = k * jnp.exp2(-b)
    k_in = k_inf.astype(BF16)
    k_dec = (k_inf * decay).astype(BF16)
    zero = jnp.zeros_like(q_in)
    qs = jnp.concatenate([jnp.where(head0, q_in, zero), jnp.where(head0, zero, q_in)], axis=1)
    att = jnp.einsum("cqd,ckd->cqk", qs, k_in, preferred_element_type=F32)
    att = jnp.where(keep2, att, 0.0).astype(BF16)
    intra = jnp.einsum("cqk,ckv->cqv", att, v, preferred_element_type=F32)
    ks = jnp.concatenate([jnp.where(head0, k_dec, zero), jnp.where(head0, zero, k_dec)], axis=1)
    vs = jnp.concatenate([v[:, :, :GLA_DV], v[:, :, GLA_DV:]], axis=1)
    kv_t = jnp.einsum("ctv,ctl->cvl", vs, ks, preferred_element_type=F32)

    state_t = state_ref[...]
    states = [None] * nch
    for c in (range(nch) if forward else range(nch - 1, -1, -1)):
        states[c] = state_t.astype(BF16)
        state_t = state_t * decay[c] + kv_t[c]
    state_ref[...] = state_t
    inter = jnp.einsum("cql,cvl->cqv", qs, jnp.stack(states), preferred_element_type=F32)
    o = jnp.concatenate([intra[:, :C, :GLA_DV] + inter[:, :C], intra[:, C:, GLA_DV:] + inter[:, C:]],
                        axis=2).reshape(TS, 2 * GLA_DV)
    o_ref[0] = o.astype(o_ref.dtype)


def _gla_pass(gq, gk, la, gv, ts, forward):
    B, S, _ = gq.shape
    nt = S // ts
    tile = (lambda i: i) if forward else (lambda i: nt - 1 - i)
    la_off = 0 if forward else GLA_KEY_WIDTH // LANES
    key = pl.BlockSpec((1, ts, LANES), lambda b, h, i: (b, tile(i), h))
    gate = pl.BlockSpec((1, ts, LANES), lambda b, h, i: (b, tile(i), la_off + h))
    val = pl.BlockSpec((1, ts, 2 * GLA_DV), lambda b, h, i: (b, tile(i), h))
    in_specs = [key, key, gate, val]
    args = [gq, gk, la, gv]
    return pl.pallas_call(
        functools.partial(_gla_kernel, TS=ts, forward=forward),
        grid=(B, N_HEADS_GLA // 2, nt),
        in_specs=in_specs,
        out_specs=val,
        out_shape=jax.ShapeDtypeStruct((B, S, GLA_WIDTH), BF16),
        scratch_shapes=[pltpu.VMEM((GLA_DV, 2 * GLA_DK), F32)],
        compiler_params=pltpu.CompilerParams(
            dimension_semantics=("parallel", "parallel", "arbitrary"), vmem_limit_bytes=VMEM_LIMIT),
        name="gla_fwd" if forward else "gla_bwd",
    )(*args)


def _tail_kernel(x_ref, od_ref, of_ref, ob_ref, r_ref, gg_ref, wo_ref, gf_ref, wg_ref, wu_ref, wd_ref, gl_ref,
                 y_ref, *, ff_chunk):
    tot = of_ref[...].astype(F32) + ob_ref[...].astype(F32)
    r = r_ref[...].astype(F32)
    gate = r / (1.0 + jnp.exp(-r))
    og = jnp.concatenate(
        [(_rms(tot[:, hh * GLA_DV:(hh + 1) * GLA_DV], gg_ref[...])
          * gate[:, hh * GLA_DV:(hh + 1) * GLA_DV]).astype(BF16) for hh in range(N_HEADS_GLA)], axis=1)
    mix = (jnp.dot(od_ref[...], wo_ref[:DIL_WIDTH, :], preferred_element_type=F32)
           + jnp.dot(og, wo_ref[DIL_WIDTH:, :], preferred_element_type=F32))
    x1 = x_ref[...] + mix
    h = _rms(x1, gf_ref[...]).astype(BF16)
    acc = x1
    for c0 in range(0, D_FF, ff_chunk):
        gate = jnp.dot(h, wg_ref[:, c0:c0 + ff_chunk], preferred_element_type=F32)
        up = jnp.dot(h, wu_ref[:, c0:c0 + ff_chunk], preferred_element_type=F32)
        act = (gate / (1.0 + jnp.exp(-gate)) * up).astype(BF16)
        acc = acc + jnp.dot(act, wd_ref[c0:c0 + ff_chunk, :], preferred_element_type=F32)
    y_ref[...] = _rms(acc, gl_ref[...])


def _tail(x, o_dil, o_f, o_b, gr, g_gla, w_out, g_ffn, w_gate, w_up, w_down, g_final, tm, ff_chunk):
    N, D = x.shape
    tok = lambda w: pl.BlockSpec((tm, w), lambda i: (i, 0))
    full = lambda a: pl.BlockSpec(a.shape, lambda i: (0, 0), pipeline_mode=pl.Buffered(1))
    return pl.pallas_call(
        functools.partial(_tail_kernel, ff_chunk=ff_chunk),
        grid=(N // tm,),
        in_specs=[tok(D), tok(DIL_WIDTH), tok(GLA_WIDTH), tok(GLA_WIDTH), tok(GLA_WIDTH), full(g_gla),
                  full(w_out), full(g_ffn), full(w_gate), full(w_up), full(w_down), full(g_final)],
        out_specs=tok(D),
        out_shape=jax.ShapeDtypeStruct((N, D), F32),
        compiler_params=pltpu.CompilerParams(
            dimension_semantics=("parallel",), vmem_limit_bytes=VMEM_LIMIT),
        name="tail",
    )(x, o_dil, o_f, o_b, gr, g_gla, w_out, g_ffn, w_gate, w_up, w_down, g_final)


def _trunk(x, p, tables, tm_in=1024, ts_gla=4096, tm_tail=1024, ff_chunk=256):
    B, S, D = x.shape
    qkv_nat, qkv_d4, qkv_d16, gq, gk, gv, gr, la = _inproj(
        x, p["g_mix"], p["w_in"], p["w_gate"], p["b_gate"], tables, tm_in)
    o_dil = _dilated_mixture(qkv_nat, qkv_d4, qkv_d16)
    ob = _gla_pass(gq, gk, la, gv, ts_gla, forward=False)
    of = _gla_pass(gq, gk, la, gv, ts_gla, forward=True)
    flat = lambda t: t.reshape(B * S, GLA_WIDTH)
    y = _tail(x.reshape(B * S, D), o_dil.reshape(B * S, DIL_WIDTH), flat(of), flat(ob), flat(gr), p["g_gla"],
              p["w_out"], p["g_ffn"], p["w_ffn_gate"], p["w_ffn_up"], p["w_ffn_down"], p["g_final"],
              tm_tail, ff_chunk)
    return y.reshape(B, S, D)


def _prepare_params(g_mix, w_in, w_gate_f, b_gate_f, w_gate_b, b_gate_b, g_gla, w_out,
                    g_ffn, w_ffn_gate, w_ffn_up, w_ffn_down, g_final):
    assert w_in.shape[0] == 1, "single-layer stack expected"
    z = jnp.zeros_like(w_gate_f[0])
    w_gate = jnp.concatenate(
        [jnp.concatenate([w_gate_f[0], z], axis=1), jnp.concatenate([z, w_gate_b[0]], axis=1)], axis=0)
    return {
        "g_mix": g_mix[0][None, :],
        "w_in": w_in[0].astype(BF16),
        "w_gate": w_gate.astype(BF16),
        "b_gate": jnp.concatenate([b_gate_f[0], b_gate_b[0]])[None, :],
        "g_gla": g_gla[0][None, :],
        "w_out": w_out[0].astype(BF16),
        "g_ffn": g_ffn[0][None, :],
        "w_ffn_gate": w_ffn_gate[0].astype(BF16),
        "w_ffn_up": w_ffn_up[0].astype(BF16),
        "w_ffn_down": w_ffn_down[0].astype(BF16),
        "g_final": g_final[None, :],
    }


def kernel(x_prompt, x_sample, g_mix, w_in, w_gate_f, b_gate_f, w_gate_b, b_gate_b, g_gla, w_out,
           g_ffn, w_ffn_gate, w_ffn_up, w_ffn_down, g_final):
    p = _prepare_params(g_mix, w_in, w_gate_f, b_gate_f, w_gate_b, b_gate_b, g_gla, w_out,
                        g_ffn, w_ffn_gate, w_ffn_up, w_ffn_down, g_final)
    tables = _rotary_tables(max(x_prompt.shape[1], x_sample.shape[1]))
    return (_trunk(x_prompt, p, tables), _trunk(x_sample, p, tables))
```
